```python
import jax, jax.numpy as jnp
from jax import lax
import numpy as np

D_MODEL = 1024
BATCH = 8
SEQ = 4096
DEPTH = 1

N_META = 16
CHUNK = 64
N_PAD = CHUNK - N_META
M_INNER = 2 * D_MODEL
M_HEADS = 4
M_HEAD_DIM = M_INNER // M_HEADS
M_QKV_BLOCK = 4
M_CONV = 4
R_HEADS = 4
R_QK_DIM = D_MODEL // R_HEADS
R_V_DIM = 2 * R_QK_DIM
R_QK = R_HEADS * R_QK_DIM
R_V = R_HEADS * R_V_DIM
D_FF = 4 * D_MODEL
ROPE_BASE = 10000.0
EPS = 1e-6
NEG = -1e30
IN_SIZES = (M_INNER, M_INNER, R_QK, R_QK, R_V, R_V, D_MODEL, D_MODEL)
N_IN = 2 * M_INNER + 2 * R_QK + 2 * R_V + 2 * D_MODEL

kernel_name = "hybrid_mlstm_retention_gated_block"


def rmsnorm(x, g):
    xf = x.astype(jnp.float32)
    y = xf * lax.rsqrt(jnp.mean(xf * xf, axis=-1, keepdims=True) + EPS)
    return (y * g.astype(jnp.float32)).astype(x.dtype)


def head_norm(h, w):
    hf = h.astype(jnp.float32)
    mu = jnp.mean(hf, axis=-1, keepdims=True)
    var = jnp.mean(jnp.square(hf - mu), axis=-1, keepdims=True)
    y = (hf - mu) * lax.rsqrt(var + EPS)
    return y.reshape(h.shape[:2] + (-1,)) * w.astype(jnp.float32)


def causal_conv(x, w, b):
    c = x.shape[-1]
    y = lax.conv_general_dilated(x, w[:, None, :].astype(x.dtype), window_strides=(1,),
                                 padding=[(M_CONV - 1, 0)],
                                 dimension_numbers=('NWC', 'WIO', 'NWC'),
                                 feature_group_count=c)
    return y + b.astype(x.dtype)


def headwise_linear(x, w):
    nb, bi, bo = w.shape
    xb = x.reshape(x.shape[:2] + (nb, bi))
    return jnp.einsum('blni,nio->blno', xb, w.astype(x.dtype)).reshape(x.shape[:2] + (nb * bo,))


def rotary(t, pos):
    half = t.shape[-1] // 2
    inv = ROPE_BASE ** (-jnp.arange(half, dtype=jnp.float32) / half)
    ang = pos.astype(jnp.float32)[:, None] * inv[None, :]
    cos = jnp.cos(ang)[None, :, None, :]
    sin = jnp.sin(ang)[None, :, None, :]
    t1, t2 = t[..., :half], t[..., half:]
    return jnp.concatenate([t1 * cos - t2 * sin, t1 * sin + t2 * cos], axis=-1).astype(t.dtype)


def to_chunks(t):
    nb, L, nh, d = t.shape
    return t.reshape(nb, L // CHUNK, CHUNK, nh, d).transpose(1, 0, 3, 2, 4)


def from_chunks(t):
    nc, nb, nh, c, d = t.shape
    return t.transpose(1, 0, 3, 2, 4).reshape(nb, nc * c, nh, d)


def gates_to_chunks(g):
    nb, L, nh = g.shape
    return g.reshape(nb, L // CHUNK, CHUNK, nh).transpose(1, 0, 3, 2)


def mlstm_chunkwise(q, k, v, ig, lf):
    d = q.shape[-1]
    qc, kc, vc = to_chunks(q), to_chunks(k * (d ** -0.5)), to_chunks(v)
    igc, lfc = gates_to_chunks(ig), gates_to_chunks(lf)
    _, nb, nh, _, _ = qc.shape
    causal = jnp.tril(jnp.ones((CHUNK, CHUNK), dtype=bool))

    def step(carry, inp):
        c_prev, n_prev, m_prev = carry
        qb, kb, vb, igb, lfb = inp
        cum_f = jnp.cumsum(lfb, axis=-1)
        g_tot = cum_f[..., -1]
        log_d = cum_f[..., :, None] - cum_f[..., None, :] + igb[..., None, :]
        log_d = jnp.where(causal, log_d, NEG)
        inter = cum_f + m_prev[..., None]
        m_row = jnp.maximum(inter, jnp.max(log_d, axis=-1))
        s = jnp.einsum('bhtd,bhsd->bhts', qb, kb) * jnp.exp(log_d - m_row[..., None])
        w_inter = jnp.exp(inter - m_row)
        num = (jnp.einsum('bhts,bhse->bhte', s, vb)
               + w_inter[..., None] * jnp.einsum('bhtd,bhde->bhte', qb, c_prev))
        den = jnp.sum(s, axis=-1) + w_inter * jnp.einsum('bhtd,bhd->bht', qb, n_prev)
        h = num / jnp.maximum(jnp.abs(den), jnp.exp(-m_row))[..., None]
        a = g_tot[..., None] - cum_f + igb
        m_new = jnp.maximum(g_tot + m_prev, jnp.max(a, axis=-1))
        w_k = jnp.exp(a - m_new[..., None])
        decay = jnp.exp(g_tot + m_prev - m_new)
        c_new = decay[..., None, None] * c_prev + jnp.einsum('bhs,bhsd,bhse->bhde', w_k, kb, vb)
        n_new = decay[..., None] * n_prev + jnp.einsum('bhs,bhsd->bhd', w_k, kb)
        return (c_new, n_new, m_new), h

    init = (jnp.zeros((nb, nh, d, d), jnp.float32), jnp.zeros((nb, nh, d), jnp.float32),
            jnp.zeros((nb, nh), jnp.float32))
    _, hs = lax.scan(step, init, (qc, kc, vc, igc, lfc))
    return from_chunks(hs)


def retention_chunkwise(q, k, v):
    nb = q.shape[0]
    dk, dv = q.shape[-1], v.shape[-1]
    qc, kc, vc = to_chunks(q), to_chunks(k * (dk ** -0.5)), to_chunks(v)
    log_gamma = jnp.log(1.0 - 2.0 ** (-5.0 - jnp.arange(R_HEADS, dtype=jnp.float32)))
    idx = jnp.arange(CHUNK, dtype=jnp.float32)
    diff = idx[:, None] - idx[None, :]
    d_mask = jnp.where(diff >= 0, jnp.exp(log_gamma[:, None, None] * diff), 0.0)
    query_decay = jnp.exp(log_gamma[:, None] * (idx + 1.0))
    key_decay = jnp.exp(log_gamma[:, None] * (CHUNK - 1.0 - idx))
    chunk_decay = jnp.exp(log_gamma * CHUNK)

    def step(state, inp):
        qb, kb, vb = inp
        s = jnp.einsum('bhtd,bhsd->bhts', qb, kb) * d_mask
        o = (jnp.einsum('bhts,bhse->bhte', s, vb)
             + query_decay[:, :, None] * jnp.einsum('bhtd,bhde->bhte', qb, state))
        state = chunk_decay[:, None, None] * state + jnp.einsum('bhsd,hs,bhse->bhde', kb, key_decay, vb)
        return state, o

    init = jnp.zeros((nb, R_HEADS, dk, dv), jnp.float32)
    _, os_ = lax.scan(step, init, (qc, kc, vc))
    return from_chunks(os_)


def mlstm_branch(xm, zm, conv_w, conv_b, w_q, w_k, w_v, w_if, b_if, skip, gn_w, valid):
    nb, L, _ = xm.shape
    xc = jax.nn.silu(causal_conv(xm, conv_w, conv_b))
    q = headwise_linear(xc, w_q)
    k = headwise_linear(xc, w_k)
    v = headwise_linear(xm, w_v)
    w_if = w_if.astype(xm.dtype)
    gates = (q @ w_if[:M_INNER] + k @ w_if[M_INNER:2 * M_INNER] + v @ w_if[2 * M_INNER:]).astype(jnp.float32) + b_if.astype(jnp.float32)
    ig, fg = gates[..., :M_HEADS], gates[..., M_HEADS:]
    ig = jnp.where(valid[None, :, None], ig, NEG)
    lf = jax.nn.log_sigmoid(fg)
    shp = (nb, L, M_HEADS, M_HEAD_DIM)
    h = mlstm_chunkwise(q.reshape(shp), k.reshape(shp), v.reshape(shp), ig, lf)
    h = head_norm(h, gn_w)
    h = (h + skip.astype(jnp.float32) * xc.astype(jnp.float32)) * jax.nn.silu(zm.astype(jnp.float32))
    return h.astype(xm.dtype)


def retention_branch(qr, kr, vr, gr, gn_w, pos, valid):
    nb, L, _ = qr.shape
    vmask = valid.astype(kr.dtype)[None, :, None, None]
    q = rotary(qr.reshape(nb, L, R_HEADS, R_QK_DIM), pos)
    k = rotary(kr.reshape(nb, L, R_HEADS, R_QK_DIM), pos) * vmask
    v = vr.reshape(nb, L, R_HEADS, R_V_DIM)
    o = retention_chunkwise(q, k, v)
    o = head_norm(o, gn_w) * jax.nn.silu(gr.astype(jnp.float32))
    return o.astype(qr.dtype)


def setup_inputs(seed: int = 0) -> dict:
    key = jax.random.key(seed)
    ks = jax.random.split(key, 24)
    f32 = jnp.float32

    def nrm(k, shape, scale):
        return jax.random.normal(k, shape, f32) * scale

    def gain(k, shape):
        return 1.0 + 0.02 * jax.random.normal(k, shape, f32)

    nblk = M_INNER // M_QKV_BLOCK
    f_bias = jnp.linspace(3.0, 6.0, M_HEADS, dtype=f32)
    b_if = jnp.concatenate([nrm(ks[10], (DEPTH, M_HEADS), 0.1),
                            f_bias[None, :] + nrm(ks[11], (DEPTH, M_HEADS), 0.1)], axis=-1)
    return {
        'x': nrm(ks[0], (BATCH, SEQ, D_MODEL), 1.0),
        'meta_tokens': nrm(ks[1], (N_META, D_MODEL), 1.0),
        'norm_mix_pre': gain(ks[2], (DEPTH, D_MODEL)),
        'w_in': nrm(ks[3], (DEPTH, D_MODEL, N_IN), D_MODEL ** -0.5),
        'conv_w': nrm(ks[4], (DEPTH, M_CONV, M_INNER), M_CONV ** -0.5),
        'conv_b': nrm(ks[5], (DEPTH, M_INNER), 0.02),
        'w_q_m': nrm(ks[6], (DEPTH, nblk, M_QKV_BLOCK, M_QKV_BLOCK), M_QKV_BLOCK ** -0.5),
        'w_k_m': nrm(ks[7], (DEPTH, nblk, M_QKV_BLOCK, M_QKV_BLOCK), M_QKV_BLOCK ** -0.5),
        'w_v_m': nrm(ks[8], (DEPTH, nblk, M_QKV_BLOCK, M_QKV_BLOCK), M_QKV_BLOCK ** -0.5),
        'w_if': nrm(ks[9], (DEPTH, 3 * M_INNER, 2 * M_HEADS), 0.1 * (3 * M_INNER) ** -0.5),
        'b_if': b_if,
        'skip_m': gain(ks[12], (DEPTH, M_INNER)),
        'gn_m': gain(ks[13], (DEPTH, M_INNER)),
        'gn_r': gain(ks[14], (DEPTH, R_V)),
        'w_proj_m': nrm(ks[15], (DEPTH, M_INNER, D_MODEL), M_INNER ** -0.5),
        'w_proj_r': nrm(ks[16], (DEPTH, R_V, D_MODEL), R_V ** -0.5),
        'w_out': nrm(ks[17], (DEPTH, D_MODEL, D_MODEL), D_MODEL ** -0.5),
        'norm_mix_post': gain(ks[18], (DEPTH, D_MODEL)),
        'norm_ffn_pre': gain(ks[19], (DEPTH, D_MODEL)),
        'w_up': nrm(ks[20], (DEPTH, D_MODEL, D_FF), D_MODEL ** -0.5),
        'w_down': nrm(ks[21], (DEPTH, D_FF, D_MODEL), D_FF ** -0.5),
        'norm_ffn_post': gain(ks[22], (DEPTH, D_MODEL)),
    }


def reference(x, meta_tokens, norm_mix_pre, w_in, conv_w, conv_b, w_q_m, w_k_m, w_v_m, w_if, b_if,
              skip_m, gn_m, gn_r, w_proj_m, w_proj_r, w_out, norm_mix_post, norm_ffn_pre, w_up,
              w_down, norm_ffn_post):
    dt = x.dtype
    nb = x.shape[0]
    meta = jnp.broadcast_to(meta_tokens.astype(dt)[None], (nb, N_META, D_MODEL))
    pad = jnp.zeros((nb, N_PAD, D_MODEL), dt)
    h = jnp.concatenate([pad, meta, x], axis=1)
    L = h.shape[1]
    pos = jnp.arange(L, dtype=jnp.int32) - N_PAD
    valid = pos >= 0
    in_mask = valid.astype(dt)[None, :, None]
    offsets = [int(o) for o in np.cumsum(IN_SIZES)[:-1]]
    for l in range(DEPTH):
        u = rmsnorm(h, norm_mix_pre[l]) * in_mask
        proj = u @ w_in[l].astype(dt)
        xm, zm, qr, kr, vr, gr, ga, gb = jnp.split(proj, offsets, axis=-1)
        ya = mlstm_branch(xm, zm, conv_w[l], conv_b[l], w_q_m[l], w_k_m[l], w_v_m[l], w_if[l],
                          b_if[l], skip_m[l], gn_m[l], valid) @ w_proj_m[l].astype(dt)
        yb = retention_branch(qr, kr, vr, gr, gn_r[l], pos, valid) @ w_proj_r[l].astype(dt)
        mix = (jax.nn.sigmoid(ga) * ya + jax.nn.sigmoid(gb) * yb) @ w_out[l].astype(dt)
        h = h + rmsnorm(mix, norm_mix_post[l])
        u = rmsnorm(h, norm_ffn_pre[l])
        f = jnp.square(jax.nn.relu(u @ w_up[l].astype(dt))) @ w_down[l].astype(dt)
        h = h + rmsnorm(f, norm_ffn_post[l])
    return h[:, CHUNK:].astype(dt)
```

```python
import functools
import math

import jax
import jax.numpy as jnp
from jax import lax
from jax.experimental import pallas as pl
from jax.experimental.pallas import tpu as pltpu

F32 = jnp.float32
BF16 = jnp.bfloat16

D_MODEL = 1024
N_META = 16
M_INNER = 2 * D_MODEL
M_HEADS = 4
M_HEAD_DIM = M_INNER // M_HEADS
M_QKV_BLOCK = 4
M_CONV = 4
R_HEADS = 4
R_QK_DIM = D_MODEL // R_HEADS
R_V_DIM = 2 * R_QK_DIM
R_QK = R_HEADS * R_QK_DIM
R_V = R_HEADS * R_V_DIM
D_FF = 4 * D_MODEL
ROPE_BASE = 10000.0
EPS = 1e-6
NEG = -1e30
LOG2E = math.log2(math.e)
N_IN = 2 * M_INNER + 2 * R_QK + 2 * R_V + 2 * D_MODEL

LANES = 128
SUBLANES = 8
MXU_DIM = 256
VMEM_LIMIT_BYTES = 56 * 1024 * 1024

CHUNK_T = 256
META_T = 128
BD_TILE = MXU_DIM
GATE_PAD = MXU_DIM
PREP_TP = 512
CONV_STRIP = 256
PROJ_TM = 512
PROJ_TM_WIDE = 1024
OUT_TM = 256
FF_TILE = 1024

_NT = (((1,), (1,)), ((), ()))
_TN = (((0,), (0,)), ((), ()))


def _const_spec(shape):
    nd = len(shape)
    return pl.BlockSpec(shape, lambda *_: (0,) * nd, pipeline_mode=pl.Buffered(1))


def _sigmoid(x):
    return 1.0 / (1.0 + jnp.exp2(x * (-LOG2E)))


def _silu(x):
    return x * _sigmoid(x)


def _normed(x_ref, g_ref):
    x = x_ref[...]
    ms = jnp.mean(x * x, axis=-1, keepdims=True)
    return (x * lax.rsqrt(ms + EPS) * g_ref[...]).astype(BF16)


def _inproj_mlstm_kernel(x_ref, g_ref, w_ref, cw_ref, cb_ref, t0_ref,
                         xm_ref, xc_ref, zs_ref, to_ref, halo_s, res_s, *, tm, tiles_per_seq):
    i = pl.program_id(0)

    @pl.when(i % tiles_per_seq == 0)
    def _():
        halo_s[...] = t0_ref[...]

    u = _normed(x_ref, g_ref)
    nstrip = M_INNER // CONV_STRIP
    nv = tm // SUBLANES
    sub = lax.broadcasted_iota(jnp.int32, (nv, SUBLANES, CONV_STRIP), 1)

    def strip_dot(c):
        sl = slice(c * CONV_STRIP, (c + 1) * CONV_STRIP)
        res_s[(i + c) % 2] = jnp.dot(u, w_ref[:, sl], preferred_element_type=F32)

    strip_dot(0)
    for c in range(nstrip):
        sl = slice(c * CONV_STRIP, (c + 1) * CONV_STRIP)
        if c + 1 < nstrip:
            strip_dot(c + 1)
        zsl = slice(M_INNER + c * CONV_STRIP, M_INNER + (c + 1) * CONV_STRIP)
        zm = jnp.dot(u, w_ref[:, zsl], preferred_element_type=F32)
        zs_ref[:, sl] = _silu(zm).astype(BF16)
        xm = res_s[(i + c) % 2]
        xm_ref[:, sl] = xm.astype(BF16)
        x3 = xm.reshape(nv, SUBLANES, CONV_STRIP)
        prev3 = jnp.concatenate([halo_s[:, sl].reshape(1, SUBLANES, CONV_STRIP), x3[:nv - 1]], axis=0)
        y = cb_ref[:, sl] + cw_ref[M_CONV - 1:M_CONV, sl] * x3
        for j in range(M_CONV - 1):
            sh = M_CONV - 1 - j
            merged = jnp.where(sub >= SUBLANES - sh, prev3, x3)
            y = y + cw_ref[j:j + 1, sl] * pltpu.roll(merged, sh, axis=1)
        halo_s[:, sl] = xm[tm - SUBLANES:, :]
        xc_ref[:, sl] = _silu(y.reshape(tm, CONV_STRIP)).astype(BF16)

    @pl.when(i == pl.num_programs(0) - 1)
    def _():
        to_ref[...] = halo_s[...]


def _inproj_rot_kernel(x_ref, g_ref, w_ref, cos_ref, sin_ref, q_ref, k_ref, *, tm, n_pad):
    u = _normed(x_ref, g_ref)
    cos = cos_ref[...]
    sin = sin_ref[...]
    half = R_QK_DIM // 2
    kscale = R_QK_DIM ** -0.5
    if n_pad:
        kmul = (lax.broadcasted_iota(jnp.int32, (tm, 1), 0) >= n_pad).astype(F32) * kscale

    def rot(t, h):
        t1 = t[:, h * R_QK_DIM:h * R_QK_DIM + half]
        t2 = t[:, h * R_QK_DIM + half:(h + 1) * R_QK_DIM]
        return t1 * cos - t2 * sin, t1 * sin + t2 * cos

    q = jnp.dot(u, w_ref[:, 0:R_QK], preferred_element_type=F32)
    for h in range(R_HEADS):
        a, b = rot(q, h)
        q_ref[:, h * R_QK_DIM:h * R_QK_DIM + half] = a.astype(BF16)
        q_ref[:, h * R_QK_DIM + half:(h + 1) * R_QK_DIM] = b.astype(BF16)
    k = jnp.dot(u, w_ref[:, R_QK:2 * R_QK], preferred_element_type=F32)
    for h in range(R_HEADS):
        a, b = rot(k, h)
        if n_pad:
            a, b = a * kmul, b * kmul
        else:
            a, b = a * kscale, b * kscale
        k_ref[:, h * R_QK_DIM:h * R_QK_DIM + half] = a.astype(BF16)
        k_ref[:, h * R_QK_DIM + half:(h + 1) * R_QK_DIM] = b.astype(BF16)


def _inproj_vg_kernel(x_ref, g_ref, w_ref, v_ref, gs_ref):
    u = _normed(x_ref, g_ref)
    v_ref[...] = jnp.dot(u, w_ref[:, 0:R_V], preferred_element_type=F32).astype(BF16)
    gs_ref[...] = _silu(jnp.dot(u, w_ref[:, R_V:2 * R_V], preferred_element_type=F32)).astype(BF16)


def _inproj_gate_kernel(x_ref, g_ref, w_ref, ga_ref, gb_ref):
    u = _normed(x_ref, g_ref)
    ga_ref[...] = _sigmoid(jnp.dot(u, w_ref[:, 0:D_MODEL], preferred_element_type=F32)).astype(BF16)
    gb_ref[...] = _sigmoid(
        jnp.dot(u, w_ref[:, D_MODEL:2 * D_MODEL], preferred_element_type=F32)).astype(BF16)


def _in_proj_call(body, name, x2d, gain, w_grp, extra_in, extra_specs, out_widths, tm,
                  extra_out=(), extra_out_specs=(), scratch=()):
    rows = x2d.shape[0]
    row_spec = lambda w: pl.BlockSpec((tm, w), lambda i: (i, 0))
    return pl.pallas_call(
        body,
        grid=(rows // tm,),
        in_specs=[row_spec(D_MODEL), _const_spec(gain.shape), _const_spec(w_grp.shape),
                  *extra_specs],
        out_specs=[*[row_spec(w) for w in out_widths], *extra_out_specs],
        out_shape=[*[jax.ShapeDtypeStruct((rows, w), BF16) for w in out_widths], *extra_out],
        scratch_shapes=list(scratch),
        compiler_params=pltpu.CompilerParams(
            dimension_semantics=("arbitrary",),
            vmem_limit_bytes=VMEM_LIMIT_BYTES),
        name=name,
    )(x2d, gain, w_grp, *extra_in)


def _in_proj_all(x2d, gain, w_bf, cw, cb, tail0, cos, sin, tm, tiles_per_seq, n_pad):
    o = 0
    w_m = w_bf[:, o:o + 2 * M_INNER]; o += 2 * M_INNER
    w_qk = w_bf[:, o:o + 2 * R_QK]; o += 2 * R_QK
    w_vg = w_bf[:, o:o + 2 * R_V]; o += 2 * R_V
    w_g = w_bf[:, o:o + 2 * D_MODEL]
    tail_shape = (SUBLANES, M_INNER)
    xm, xc, zs, tail = _in_proj_call(
        functools.partial(_inproj_mlstm_kernel, tm=tm, tiles_per_seq=tiles_per_seq),
        "in_proj_mlstm", x2d, gain, w_m, (cw, cb, tail0),
        (_const_spec(cw.shape), _const_spec(cb.shape), _const_spec(tail_shape)),
        (M_INNER, M_INNER, M_INNER), tm,
        extra_out=(jax.ShapeDtypeStruct(tail_shape, F32),),
        extra_out_specs=(pl.BlockSpec(tail_shape, lambda i: (0, 0)),),
        scratch=(pltpu.VMEM(tail_shape, F32), pltpu.VMEM((2, tm, CONV_STRIP), F32)))
    half = R_QK_DIM // 2
    tmw = min(tm * PROJ_TM_WIDE // PROJ_TM, x2d.shape[0])
    tps_w = tiles_per_seq * tm // tmw
    pos_spec = pl.BlockSpec((tmw, half), lambda i: (i % tps_w, 0))
    qr, kr = _in_proj_call(
        functools.partial(_inproj_rot_kernel, tm=tmw, n_pad=n_pad),
        "in_proj_rot", x2d, gain, w_qk, (cos, sin), (pos_spec, pos_spec), (R_QK, R_QK), tmw)
    v, gs = _in_proj_call(_inproj_vg_kernel, "in_proj_vg", x2d, gain, w_vg, (), (), (R_V, R_V), tm)
    ga, gb = _in_proj_call(_inproj_gate_kernel, "in_proj_gate", x2d, gain, w_g, (), (),
                           (D_MODEL, D_MODEL), tmw)
    return (xm, xc, zs, tail), (qr, kr, v, gs), (ga, gb)


def _head_norm(h, w):
    mu = jnp.mean(h, axis=-1, keepdims=True)
    c = h - mu
    var = jnp.mean(c * c, axis=-1, keepdims=True)
    return c * lax.rsqrt(var + EPS) * w


def _lane_cumsum(x, n):
    lane = lax.broadcasted_iota(jnp.int32, x.shape, 1)
    d = 1
    while d < n:
        x = x + jnp.where(lane >= d, pltpu.roll(x, d, axis=1), 0.0)
        d *= 2
    return x


def _mlstm_prep_kernel(xm_ref, xc_ref, bdq_ref, bdk_ref, bdv_ref, wif_ref, bif_ref,
                       q_ref, k_ref, v_ref, row_ref, col_ref, *, TP, T, n_pad):
    scale = M_HEAD_DIM ** -0.5
    ntile = M_INNER // BD_TILE
    qs, ks, vs = [], [], []
    for j in range(ntile):
        sl = slice(j * BD_TILE, (j + 1) * BD_TILE)
        qj = jnp.dot(xc_ref[:, sl], bdq_ref[j], preferred_element_type=F32)
        kj = jnp.dot(xc_ref[:, sl], bdk_ref[j], preferred_element_type=F32)
        vj = jnp.dot(xm_ref[:, sl], bdv_ref[j], preferred_element_type=F32)
        qs.append(qj.astype(BF16))
        ks.append(kj.astype(BF16))
        vs.append(vj.astype(BF16))
        q_ref[:, sl] = qs[-1]
        k_ref[:, sl] = (kj * scale).astype(BF16)
        v_ref[:, sl] = vs[-1]
    qkv = jnp.concatenate(qs + ks + vs, axis=1)
    g_all = jnp.dot(qkv, wif_ref[...], preferred_element_type=F32)[:, 0:LANES] + bif_ref[...]
    lane = lax.broadcasted_iota(jnp.int32, (T, LANES), 1)
    subl = lax.broadcasted_iota(jnp.int32, (SUBLANES, T), 0)
    for c in range(TP // T):
        g = g_all[c * T:(c + 1) * T, :]
        if n_pad:
            rowv = lax.broadcasted_iota(jnp.int32, (T, LANES), 0) >= n_pad
            g = jnp.where(jnp.logical_and(lane < M_HEADS, jnp.logical_not(rowv)), NEG, g)
        gt = g.T
        lf_t = jnp.minimum(gt, 0.0) - jnp.log1p(jnp.exp(-jnp.abs(gt)))
        cum_t = _lane_cumsum(lf_t, T)
        row_ref[c] = jnp.where(subl < M_HEADS, gt[0:SUBLANES, :], cum_t[0:SUBLANES, :])
        col_ref[c * T:(c + 1) * T, :] = jnp.where(lane < M_HEADS, g, cum_t.T)


def _mlstm_prep(xm, xc, weights, TP, T, n_pad):
    nb, L, _ = xm.shape
    blk = pl.BlockSpec((None, TP, M_INNER), lambda b, i: (b, i, 0))
    big = jax.ShapeDtypeStruct((nb, L, M_INNER), BF16)
    return pl.pallas_call(
        functools.partial(_mlstm_prep_kernel, TP=TP, T=T, n_pad=n_pad),
        grid=(nb, L // TP),
        in_specs=[blk, blk, *[_const_spec(w.shape) for w in weights]],
        out_specs=[blk, blk, blk,
                   pl.BlockSpec((None, TP // T, SUBLANES, T), lambda b, i: (b, i, 0, 0)),
                   pl.BlockSpec((None, TP, LANES), lambda b, i: (b, i, 0))],
        out_shape=[big, big, big,
                   jax.ShapeDtypeStruct((nb, L // T, SUBLANES, T), F32),
                   jax.ShapeDtypeStruct((nb, L, LANES), F32)],
        compiler_params=pltpu.CompilerParams(
            dimension_semantics=("arbitrary", "arbitrary"),
            vmem_limit_bytes=VMEM_LIMIT_BYTES),
        name="mlstm_prep",
    )(xm, xc, *weights)


def _mlstm_kernel(q_ref, k_ref, v_ref, xc_ref, zs_ref, row_ref, col_ref, skip_ref, gn_ref,
                  c0_ref, n0_ref, m0_ref, h_ref, *rest, T, emit_state):
    if emit_state:
        co_ref, no_ref, mo_ref, c_s, cb_s, n_s, m_s = rest
    else:
        c_s, cb_s, n_s, m_s = rest
    ci = pl.program_id(1)
    nc = pl.num_programs(1)

    @pl.when(ci == 0)
    def _():
        c_s[...] = c0_ref[...]
        cb_s[...] = c0_ref[...].astype(BF16)
        n_s[...] = n0_ref[...]
        m_s[...] = m0_ref[...]

    rowp = row_ref[...]
    colp = col_ref[...]
    causal = (lax.broadcasted_iota(jnp.int32, (T, T), 0)
              >= lax.broadcasted_iota(jnp.int32, (T, T), 1))

    heads = range(M_HEADS)
    sls = [slice(h * M_HEAD_DIM, (h + 1) * M_HEAD_DIM) for h in heads]

    s_raw = [lax.dot_general(q_ref[:, sls[h]], k_ref[:, sls[h]], _NT, preferred_element_type=F32)
             for h in heads]
    qc = [jnp.dot(q_ref[:, sls[h]], cb_s[h], preferred_element_type=F32) for h in heads]

    dmat, w_inter, e_negm, w_k, decay, m_new = [], [], [], [], [], []
    for h in heads:
        i_row = rowp[h:h + 1, :]
        f_row = rowp[M_HEADS + h:M_HEADS + h + 1, :]
        i_col = colp[:, h:h + 1]
        f_col = colp[:, M_HEADS + h:M_HEADS + h + 1]
        g_tot = f_row[:, T - 1:T]
        m_prev = m_s[h][0:1, 0:1]
        b_row = i_row - f_row
        log_d = jnp.where(causal, f_col + b_row, NEG)
        inter = f_col + m_prev
        m_row = jnp.maximum(inter, jnp.max(log_d, axis=1, keepdims=True))
        dmat.append(jnp.exp(log_d - m_row))
        w_inter.append(jnp.exp(inter - m_row))
        e_negm.append(jnp.exp(-m_row))
        mn = jnp.maximum(g_tot + m_prev, jnp.max(g_tot + b_row, axis=1, keepdims=True))
        m_new.append(mn)
        w_k.append(jnp.exp(g_tot - f_col + i_col - mn))
        decay.append(jnp.exp(g_tot + m_prev - mn))

    for h in heads:
        kw = k_ref[:, sls[h]].astype(F32) * w_k[h]
        c_new = decay[h] * c_s[h] + lax.dot_general(kw.astype(BF16), v_ref[:, sls[h]], _TN,
                                                    preferred_element_type=F32)
        n_new = decay[h] * n_s[h][0:1, :] + jnp.sum(kw, axis=0, keepdims=True)
        qn = jnp.sum(q_ref[:, sls[h]].astype(F32) * n_s[h][0:1, :], axis=1, keepdims=True)
        s = s_raw[h] * dmat[h]
        num = (jnp.dot(s.astype(BF16), v_ref[:, sls[h]], preferred_element_type=F32)
               + w_inter[h] * qc[h])
        den = jnp.sum(s, axis=1, keepdims=True) + w_inter[h] * qn
        hh = num * (1.0 / jnp.maximum(jnp.abs(den), e_negm[h]))
        c_s[h] = c_new
        cb_s[h] = c_new.astype(BF16)
        n_s[h] = jnp.broadcast_to(n_new, (SUBLANES, M_HEAD_DIM))
        m_s[h] = jnp.broadcast_to(m_new[h], (SUBLANES, LANES))
        hn = _head_norm(hh, gn_ref[:, sls[h]])
        out = ((hn + skip_ref[:, sls[h]] * xc_ref[:, sls[h]].astype(F32))
               * zs_ref[:, sls[h]].astype(F32))
        h_ref[:, sls[h]] = out.astype(BF16)

    if emit_state:
        @pl.when(ci == nc - 1)
        def _():
            co_ref[...] = c_s[...]
            no_ref[...] = n_s[...]
            mo_ref[...] = m_s[...]


def _mlstm(q, k, v, xc, zs, rowp, colp, skip, gn, state, T, emit_state):
    nb, L, _ = q.shape
    nc = L // T
    blk = pl.BlockSpec((None, T, M_INNER), lambda b, c: (b, c, 0))
    in_specs = [blk, blk, blk, blk, blk,
                pl.BlockSpec((None, None, SUBLANES, T), lambda b, c: (b, c, 0, 0)),
                pl.BlockSpec((None, T, LANES), lambda b, c: (b, c, 0)),
                _const_spec(skip.shape), _const_spec(gn.shape),
                *[_const_spec(s.shape) for s in state]]
    out_shape = [jax.ShapeDtypeStruct((nb, L, M_INNER), BF16)]
    out_specs = [blk]
    if emit_state:
        assert nb == 1
        for s in state:
            out_shape.append(jax.ShapeDtypeStruct(s.shape, F32))
            out_specs.append(pl.BlockSpec(s.shape, lambda b, c, _n=len(s.shape): (0,) * _n))
    c_shape = state[0].shape
    scratch = [pltpu.VMEM(c_shape, F32), pltpu.VMEM(c_shape, BF16),
               pltpu.VMEM(state[1].shape, F32), pltpu.VMEM(state[2].shape, F32)]
    return pl.pallas_call(
        functools.partial(_mlstm_kernel, T=T, emit_state=emit_state),
        grid=(nb, nc),
        in_specs=in_specs,
        out_specs=out_specs,
        out_shape=out_shape,
        scratch_shapes=scratch,
        compiler_params=pltpu.CompilerParams(
            dimension_semantics=("arbitrary", "arbitrary"),
            vmem_limit_bytes=VMEM_LIMIT_BYTES),
        name="mlstm_state" if emit_state else "mlstm",
    )(q, k, v, xc, zs, rowp, colp, skip, gn, *state)


def _retention_kernel(q_ref, k_ref, v_ref, gs_ref, gn_ref, s0_ref,
                      o_ref, *rest, T, emit_state):
    if emit_state:
        so_ref, st_s, dm_s = rest
    else:
        st_s, dm_s = rest
    ci = pl.program_id(1)
    nc = pl.num_programs(1)
    log_gamma = [math.log(1.0 - 2.0 ** (-5.0 - h)) for h in range(R_HEADS)]

    @pl.when(ci == 0)
    def _():
        st_s[...] = s0_ref[...]
        diff = (lax.broadcasted_iota(jnp.int32, (T, T), 0)
                - lax.broadcasted_iota(jnp.int32, (T, T), 1))
        dfl = diff.astype(F32)
        for h in range(R_HEADS):
            dm_s[h] = jnp.where(diff >= 0, jnp.exp(log_gamma[h] * dfl), 0.0)

    tf = lax.broadcasted_iota(jnp.int32, (T, 1), 0).astype(F32)
    for h in range(R_HEADS):
        qh = q_ref[:, h * R_QK_DIM:(h + 1) * R_QK_DIM]
        kh = k_ref[:, h * R_QK_DIM:(h + 1) * R_QK_DIM]
        sl = slice(h * R_V_DIM, (h + 1) * R_V_DIM)
        vh = v_ref[:, sl]
        s = lax.dot_general(qh, kh, _NT, preferred_element_type=F32) * dm_s[h]
        q_decay = jnp.exp(log_gamma[h] * (tf + 1.0))
        k_decay = jnp.exp(log_gamma[h] * (T - 1.0 - tf))
        o = (jnp.dot(s.astype(BF16), vh, preferred_element_type=F32)
             + q_decay * jnp.dot(qh, st_s[h].astype(BF16), preferred_element_type=F32))
        kd = (kh.astype(F32) * k_decay).astype(BF16)
        st_s[h] = (math.exp(log_gamma[h] * T) * st_s[h]
                   + lax.dot_general(kd, vh, _TN, preferred_element_type=F32))
        o_ref[:, sl] = (_head_norm(o, gn_ref[:, sl]) * gs_ref[:, sl].astype(F32)).astype(BF16)

    if emit_state:
        @pl.when(ci == nc - 1)
        def _():
            so_ref[...] = st_s[...]


def _retention(q, k, v, gs, gn, s0, T, emit_state):
    nb, L, _ = q.shape
    nc = L // T
    qk_blk = pl.BlockSpec((None, T, R_QK), lambda b, c: (b, c, 0))
    v_blk = pl.BlockSpec((None, T, R_V), lambda b, c: (b, c, 0))
    in_specs = [qk_blk, qk_blk, v_blk, v_blk, _const_spec(gn.shape), _const_spec(s0.shape)]
    out_shape = [jax.ShapeDtypeStruct((nb, L, R_V), BF16)]
    out_specs = [v_blk]
    if emit_state:
        assert nb == 1
        out_shape.append(jax.ShapeDtypeStruct(s0.shape, F32))
        out_specs.append(pl.BlockSpec(s0.shape, lambda b, c: (0, 0, 0)))
    scratch = [pltpu.VMEM(s0.shape, F32), pltpu.VMEM((R_HEADS, T, T), F32)]
    return pl.pallas_call(
        functools.partial(_retention_kernel, T=T, emit_state=emit_state),
        grid=(nb, nc),
        in_specs=in_specs,
        out_specs=out_specs,
        out_shape=out_shape,
        scratch_shapes=scratch,
        compiler_params=pltpu.CompilerParams(
            dimension_semantics=("arbitrary", "arbitrary"),
            vmem_limit_bytes=VMEM_LIMIT_BYTES),
        name="retention_state" if emit_state else "retention",
    )(q, k, v, gs, gn, s0)


def _rms(x, g):
    return x * lax.rsqrt(jnp.mean(x * x, axis=-1, keepdims=True) + EPS) * g


def _out_ffn_kernel(hm_ref, hr_ref, ga_ref, gb_ref, x_ref, wpm_ref, wpr_ref, wout_ref,
                    wup_ref, wdn_ref, g1_ref, g2_ref, g3_ref, o_ref):
    ya = jnp.dot(hm_ref[...], wpm_ref[...], preferred_element_type=F32)
    yb = jnp.dot(hr_ref[...], wpr_ref[...], preferred_element_type=F32)
    mixin = ga_ref[...].astype(F32) * ya + gb_ref[...].astype(F32) * yb
    mix = jnp.dot(mixin.astype(BF16), wout_ref[...], preferred_element_type=F32)
    h1 = x_ref[...] + _rms(mix, g1_ref[...])
    u = _rms(h1, g2_ref[...]).astype(BF16)
    f = None
    for j in range(D_FF // FF_TILE):
        sl = slice(j * FF_TILE, (j + 1) * FF_TILE)
        a = jnp.maximum(jnp.dot(u, wup_ref[:, sl], preferred_element_type=F32), 0.0)
        part = jnp.dot((a * a).astype(BF16), wdn_ref[sl, :], preferred_element_type=F32)
        f = part if f is None else f + part
    o_ref[...] = h1 + _rms(f, g3_ref[...])


def _out_ffn(hm, hr, ga, gb, x2d, wpm, wpr, wout, wup, wdn, g1, g2, g3, tm):
    rows = x2d.shape[0]
    row_spec = lambda w: pl.BlockSpec((tm, w), lambda i: (i, 0))
    consts = (wpm, wpr, wout, wup, wdn, g1, g2, g3)
    return pl.pallas_call(
        _out_ffn_kernel,
        grid=(rows // tm,),
        in_specs=[row_spec(M_INNER), row_spec(R_V), row_spec(D_MODEL), row_spec(D_MODEL),
                  row_spec(D_MODEL), *[_const_spec(c.shape) for c in consts]],
        out_specs=row_spec(D_MODEL),
        out_shape=jax.ShapeDtypeStruct((rows, D_MODEL), F32),
        compiler_params=pltpu.CompilerParams(
            dimension_semantics=("arbitrary",),
            vmem_limit_bytes=VMEM_LIMIT_BYTES),
        name="out_ffn",
    )(hm, hr, ga, gb, x2d, *consts)


def _block_diag_tiles(w):
    rows = w.reshape(M_INNER, M_QKV_BLOCK)
    wide = jnp.tile(rows, (1, BD_TILE // M_QKV_BLOCK))
    r_blk = (lax.broadcasted_iota(jnp.int32, wide.shape, 0) % BD_TILE) // M_QKV_BLOCK
    c_blk = lax.broadcasted_iota(jnp.int32, wide.shape, 1) // M_QKV_BLOCK
    t = jnp.where(r_blk == c_blk, wide, 0.0).astype(BF16)
    return t.reshape(M_INNER // BD_TILE, BD_TILE, BD_TILE)


def _rope_tables(pos):
    half = R_QK_DIM // 2
    inv = ROPE_BASE ** (-jnp.arange(half, dtype=F32) / half)
    ang = pos.astype(F32)[:, None] * inv[None, :]
    return jnp.cos(ang), jnp.sin(ang)


def kernel(x, meta_tokens, norm_mix_pre, w_in, conv_w, conv_b, w_q_m, w_k_m, w_v_m, w_if, b_if,
           skip_m, gn_m, gn_r, w_proj_m, w_proj_r, w_out, norm_mix_post, norm_ffn_pre, w_up,
           w_down, norm_ffn_post):
    nb, seq, _ = x.shape
    assert norm_mix_pre.shape[0] == 1, "single-layer block"
    assert seq % PROJ_TM_WIDE == 0 and seq % CHUNK_T == 0
    row = lambda a: a[0].reshape(1, -1).astype(F32)
    n_pad = META_T - N_META

    w_in_bf = w_in[0].astype(BF16)
    g_pre = row(norm_mix_pre)
    cw, cb = conv_w[0].astype(F32), row(conv_b)

    meta_rows = jnp.concatenate(
        [jnp.zeros((n_pad, D_MODEL), x.dtype), meta_tokens.astype(x.dtype)], axis=0)
    cos_m, sin_m = _rope_tables(jnp.arange(META_T, dtype=jnp.int32) - n_pad)
    cos_x, sin_x = _rope_tables(jnp.arange(seq, dtype=jnp.int32) + N_META)
    tail_zero = jnp.zeros((SUBLANES, M_INNER), F32)
    m_meta, r_meta, _ = _in_proj_all(meta_rows, g_pre, w_in_bf, cw, cb, tail_zero, cos_m, sin_m,
                                     META_T, 1, n_pad)
    x2d = x.reshape(nb * seq, D_MODEL)
    m_in, r_in, (ga, gb) = _in_proj_all(x2d, g_pre, w_in_bf, cw, cb, m_meta[3], cos_x, sin_x,
                                        PROJ_TM, seq // PROJ_TM, 0)

    wif = jnp.zeros((3 * M_INNER, GATE_PAD), F32).at[:, :2 * M_HEADS].set(w_if[0]).astype(BF16)
    bif = jnp.zeros((1, LANES), F32).at[0, :2 * M_HEADS].set(b_if[0])
    p_weights = (_block_diag_tiles(w_q_m[0]), _block_diag_tiles(w_k_m[0]),
                 _block_diag_tiles(w_v_m[0]), wif, bif)
    m_zero = (jnp.zeros((M_HEADS, M_HEAD_DIM, M_HEAD_DIM), F32),
              jnp.zeros((M_HEADS, SUBLANES, M_HEAD_DIM), F32),
              jnp.zeros((M_HEADS, SUBLANES, LANES), F32))
    as_seq = lambda a, n, t: a.reshape(n, t, a.shape[-1])
    xm_m, xc_m, zs_m = [as_seq(a, 1, META_T) for a in m_meta[:3]]
    xm_x, xc_x, zs_x = [as_seq(a, nb, seq) for a in m_in[:3]]
    prep_m = _mlstm_prep(xm_m, xc_m, p_weights, META_T, META_T, n_pad)
    m_init = _mlstm(*prep_m[:3], xc_m, zs_m, *prep_m[3:], row(skip_m), row(gn_m), m_zero,
                    META_T, True)[1:]
    prep_x = _mlstm_prep(xm_x, xc_x, p_weights, PREP_TP, CHUNK_T, 0)
    hm = _mlstm(*prep_x[:3], xc_x, zs_x, *prep_x[3:], row(skip_m), row(gn_m), tuple(m_init),
                CHUNK_T, False)[0]

    r_zero = jnp.zeros((R_HEADS, R_QK_DIM, R_V_DIM), F32)
    r_init = _retention(*[as_seq(a, 1, META_T) for a in r_meta], row(gn_r), r_zero,
                        META_T, True)[1]
    hr = _retention(*[as_seq(a, nb, seq) for a in r_in], row(gn_r), r_init, CHUNK_T, False)[0]

    out = _out_ffn(hm.reshape(nb * seq, M_INNER), hr.reshape(nb * seq, R_V), ga, gb, x2d,
                   w_proj_m[0].astype(BF16), w_proj_r[0].astype(BF16), w_out[0].astype(BF16),
                   w_up[0].astype(BF16), w_down[0].astype(BF16),
                   row(norm_mix_post), row(norm_ffn_pre), row(norm_ffn_post), OUT_TM)
    return out.reshape(nb, seq, D_MODEL)
```

```python
import functools
import math

import jax
import jax.numpy as jnp
from jax import lax
from jax.experimental import pallas as pl
from jax.experimental.pallas import tpu as pltpu

F32 = jnp.float32
BF16 = jnp.bfloat16

D_MODEL = 1024
N_META = 16
M_INNER = 2 * D_MODEL
M_HEADS = 4
M_HEAD_DIM = M_INNER // M_HEADS
M_QKV_BLOCK = 4
M_CONV = 4
R_HEADS = 4
R_QK_DIM = D_MODEL // R_HEADS
R_V_DIM = 2 * R_QK_DIM
R_QK = R_HEADS * R_QK_DIM
R_V = R_HEADS * R_V_DIM
D_FF = 4 * D_MODEL
ROPE_BASE = 10000.0
EPS = 1e-6
NEG = -1e30
LOG2E = math.log2(math.e)
N_IN = 2 * M_INNER + 2 * R_QK + 2 * R_V + 2 * D_MODEL

LANES = 128
SUBLANES = 8
MXU_DIM = 256
VMEM_LIMIT_BYTES = 56 * 1024 * 1024

CHUNK_T = 256
META_T = 128
BD_TILE = MXU_DIM
GATE_PAD = MXU_DIM
PREP_TP = 512
CONV_STRIP = 256
PROJ_TM = 512
PROJ_TM_WIDE = 1024
OUT_TM = 512
FF_TILE = 1024

_NT = (((1,), (1,)), ((), ()))
_TN = (((0,), (0,)), ((), ()))


def _const_spec(shape):
    nd = len(shape)
    return pl.BlockSpec(shape, lambda *_: (0,) * nd, pipeline_mode=pl.Buffered(1))


def _sigmoid(x):
    return 1.0 / (1.0 + jnp.exp2(x * (-LOG2E)))


def _silu(x):
    return x * _sigmoid(x)


def _normed(x_ref, g_ref):
    x = x_ref[...]
    ms = jnp.mean(x * x, axis=-1, keepdims=True)
    return (x * lax.rsqrt(ms + EPS) * g_ref[...]).astype(BF16)


def _inproj_mlstm_kernel(x_ref, g_ref, w_ref, cw_ref, cb_ref, t0_ref,
                         xm_ref, xc_ref, zs_ref, to_ref, halo_s, res_s, *, tm, tiles_per_seq):
    i = pl.program_id(0)

    @pl.when(i % tiles_per_seq == 0)
    def _():
        halo_s[...] = t0_ref[...]

    u = _normed(x_ref, g_ref)
    nstrip = M_INNER // CONV_STRIP
    nv = tm // SUBLANES
    sub = lax.broadcasted_iota(jnp.int32, (nv, SUBLANES, CONV_STRIP), 1)

    def strip_dot(c):
        sl = slice(c * CONV_STRIP, (c + 1) * CONV_STRIP)
        res_s[(i + c) % 2] = jnp.dot(u, w_ref[:, sl], preferred_element_type=F32)

    strip_dot(0)
    for c in range(nstrip):
        sl = slice(c * CONV_STRIP, (c + 1) * CONV_STRIP)
        if c + 1 < nstrip:
            strip_dot(c + 1)
        zsl = slice(M_INNER + c * CONV_STRIP, M_INNER + (c + 1) * CONV_STRIP)
        zm = jnp.dot(u, w_ref[:, zsl], preferred_element_type=F32)
        zs_ref[:, sl] = _silu(zm).astype(BF16)
        xm = res_s[(i + c) % 2]
        xm_ref[:, sl] = xm.astype(BF16)
        x3 = xm.reshape(nv, SUBLANES, CONV_STRIP)
        prev3 = jnp.concatenate([halo_s[:, sl].reshape(1, SUBLANES, CONV_STRIP), x3[:nv - 1]], axis=0)
        y = cb_ref[:, sl] + cw_ref[M_CONV - 1:M_CONV, sl] * x3
        for j in range(M_CONV - 1):
            sh = M_CONV - 1 - j
            merged = jnp.where(sub >= SUBLANES - sh, prev3, x3)
            y = y + cw_ref[j:j + 1, sl] * pltpu.roll(merged, sh, axis=1)
        halo_s[:, sl] = xm[tm - SUBLANES:, :]
        xc_ref[:, sl] = _silu(y.reshape(tm, CONV_STRIP)).astype(BF16)

    @pl.when(i == pl.num_programs(0) - 1)
    def _():
        to_ref[...] = halo_s[...]


def _inproj_rot_kernel(x_ref, g_ref, w_ref, cos_ref, sin_ref, q_ref, k_ref, *, tm, n_pad):
    u = _normed(x_ref, g_ref)
    cos = cos_ref[...]
    sin = sin_ref[...]
    half = R_QK_DIM // 2
    kscale = R_QK_DIM ** -0.5
    if n_pad:
        kmul = (lax.broadcasted_iota(jnp.int32, (tm, 1), 0) >= n_pad).astype(F32) * kscale

    def rot(t, h):
        t1 = t[:, h * R_QK_DIM:h * R_QK_DIM + half]
        t2 = t[:, h * R_QK_DIM + half:(h + 1) * R_QK_DIM]
        return t1 * cos - t2 * sin, t1 * sin + t2 * cos

    q = jnp.dot(u, w_ref[:, 0:R_QK], preferred_element_type=F32)
    for h in range(R_HEADS):
        a, b = rot(q, h)
        q_ref[:, h * R_QK_DIM:h * R_QK_DIM + half] = a.astype(BF16)
        q_ref[:, h * R_QK_DIM + half:(h + 1) * R_QK_DIM] = b.astype(BF16)
    k = jnp.dot(u, w_ref[:, R_QK:2 * R_QK], preferred_element_type=F32)
    for h in range(R_HEADS):
        a, b = rot(k, h)
        if n_pad:
            a, b = a * kmul, b * kmul
        else:
            a, b = a * kscale, b * kscale
        k_ref[:, h * R_QK_DIM:h * R_QK_DIM + half] = a.astype(BF16)
        k_ref[:, h * R_QK_DIM + half:(h + 1) * R_QK_DIM] = b.astype(BF16)


def _inproj_vg_kernel(x_ref, g_ref, w_ref, v_ref, gs_ref):
    u = _normed(x_ref, g_ref)
    v_ref[...] = jnp.dot(u, w_ref[:, 0:R_V], preferred_element_type=F32).astype(BF16)
    gs_ref[...] = _silu(jnp.dot(u, w_ref[:, R_V:2 * R_V], preferred_element_type=F32)).astype(BF16)


def _inproj_gate_kernel(x_ref, g_ref, w_ref, ga_ref, gb_ref):
    u = _normed(x_ref, g_ref)
    ga_ref[...] = _sigmoid(jnp.dot(u, w_ref[:, 0:D_MODEL], preferred_element_type=F32)).astype(BF16)
    gb_ref[...] = _sigmoid(
        jnp.dot(u, w_ref[:, D_MODEL:2 * D_MODEL], preferred_element_type=F32)).astype(BF16)


def _in_proj_call(body, name, x2d, gain, w_grp, extra_in, extra_specs, out_widths, tm,
                  extra_out=(), extra_out_specs=(), scratch=()):
    rows = x2d.shape[0]
    row_spec = lambda w: pl.BlockSpec((tm, w), lambda i: (i, 0))
    return pl.pallas_call(
        body,
        grid=(rows // tm,),
        in_specs=[row_spec(D_MODEL), _const_spec(gain.shape), _const_spec(w_grp.shape),
                  *extra_specs],
        out_specs=[*[row_spec(w) for w in out_widths], *extra_out_specs],
        out_shape=[*[jax.ShapeDtypeStruct((rows, w), BF16) for w in out_widths], *extra_out],
        scratch_shapes=list(scratch),
        compiler_params=pltpu.CompilerParams(
            dimension_semantics=("arbitrary",),
            vmem_limit_bytes=VMEM_LIMIT_BYTES),
        name=name,
    )(x2d, gain, w_grp, *extra_in)


def _in_proj_all(x2d, gain, w_bf, cw, cb, tail0, cos, sin, tm, tiles_per_seq, n_pad):
    o = 0
    w_m = w_bf[:, o:o + 2 * M_INNER]; o += 2 * M_INNER
    w_qk = w_bf[:, o:o + 2 * R_QK]; o += 2 * R_QK
    w_vg = w_bf[:, o:o + 2 * R_V]; o += 2 * R_V
    w_g = w_bf[:, o:o + 2 * D_MODEL]
    tail_shape = (SUBLANES, M_INNER)
    xm, xc, zs, tail = _in_proj_call(
        functools.partial(_inproj_mlstm_kernel, tm=tm, tiles_per_seq=tiles_per_seq),
        "in_proj_mlstm", x2d, gain, w_m, (cw, cb, tail0),
        (_const_spec(cw.shape), _const_spec(cb.shape), _const_spec(tail_shape)),
        (M_INNER, M_INNER, M_INNER), tm,
        extra_out=(jax.ShapeDtypeStruct(tail_shape, F32),),
        extra_out_specs=(pl.BlockSpec(tail_shape, lambda i: (0, 0)),),
        scratch=(pltpu.VMEM(tail_shape, F32), pltpu.VMEM((2, tm, CONV_STRIP), F32)))
    half = R_QK_DIM // 2
    tmw = min(tm * PROJ_TM_WIDE // PROJ_TM, x2d.shape[0])
    tps_w = tiles_per_seq * tm // tmw
    pos_spec = pl.BlockSpec((tmw, half), lambda i: (i % tps_w, 0))
    qr, kr = _in_proj_call(
        functools.partial(_inproj_rot_kernel, tm=tmw, n_pad=n_pad),
        "in_proj_rot", x2d, gain, w_qk, (cos, sin), (pos_spec, pos_spec), (R_QK, R_QK), tmw)
    v, gs = _in_proj_call(_inproj_vg_kernel, "in_proj_vg", x2d, gain, w_vg, (), (), (R_V, R_V), tmw)
    ga, gb = _in_proj_call(_inproj_gate_kernel, "in_proj_gate", x2d, gain, w_g, (), (),
                           (D_MODEL, D_MODEL), tmw)
    return (xm, xc, zs, tail), (qr, kr, v, gs), (ga, gb)


def _head_norm(h, w):
    mu = jnp.mean(h, axis=-1, keepdims=True)
    c = h - mu
    var = jnp.mean(c * c, axis=-1, keepdims=True)
    return c * lax.rsqrt(var + EPS) * w


def _lane_cumsum(x, n):
    lane = lax.broadcasted_iota(jnp.int32, x.shape, 1)
    d = 1
    while d < n:
        x = x + jnp.where(lane >= d, pltpu.roll(x, d, axis=1), 0.0)
        d *= 2
    return x


def _mlstm_prep_kernel(xm_ref, xc_ref, bdq_ref, bdk_ref, bdv_ref, wif_ref, bif_ref,
                       q_ref, k_ref, v_ref, row_ref, col_ref, *, TP, T, n_pad):
    scale = M_HEAD_DIM ** -0.5
    for j in range(M_INNER // BD_TILE):
        sl = slice(j * BD_TILE, (j + 1) * BD_TILE)
        q_ref[:, sl] = jnp.dot(xc_ref[:, sl], bdq_ref[j], preferred_element_type=F32).astype(BF16)
        kj = jnp.dot(xc_ref[:, sl], bdk_ref[j], preferred_element_type=F32)
        k_ref[:, sl] = (kj * scale).astype(BF16)
        v_ref[:, sl] = jnp.dot(xm_ref[:, sl], bdv_ref[j], preferred_element_type=F32).astype(BF16)
    xcm = jnp.concatenate([xc_ref[...], xm_ref[...]], axis=1)
    g_all = jnp.dot(xcm, wif_ref[...], preferred_element_type=F32)[:, 0:LANES] + bif_ref[...]
    lane = lax.broadcasted_iota(jnp.int32, (T, LANES), 1)
    subl = lax.broadcasted_iota(jnp.int32, (SUBLANES, T), 0)
    for c in range(TP // T):
        g = g_all[c * T:(c + 1) * T, :]
        if n_pad:
            rowv = lax.broadcasted_iota(jnp.int32, (T, LANES), 0) >= n_pad
            g = jnp.where(jnp.logical_and(lane < M_HEADS, jnp.logical_not(rowv)), NEG, g)
        gt = g.T
        lf_t = jnp.minimum(gt, 0.0) - jnp.log1p(jnp.exp(-jnp.abs(gt)))
        cum_t = _lane_cumsum(lf_t, T)
        row_ref[c] = jnp.where(subl < M_HEADS, gt[0:SUBLANES, :], cum_t[0:SUBLANES, :])
        col_ref[c * T:(c + 1) * T, :] = jnp.where(lane < M_HEADS, g, cum_t.T)


def _mlstm_prep(xm, xc, weights, TP, T, n_pad):
    nb, L, _ = xm.shape
    blk = pl.BlockSpec((None, TP, M_INNER), lambda b, i: (b, i, 0))
    big = jax.ShapeDtypeStruct((nb, L, M_INNER), BF16)
    return pl.pallas_call(
        functools.partial(_mlstm_prep_kernel, TP=TP, T=T, n_pad=n_pad),
        grid=(nb, L // TP),
        in_specs=[blk, blk, *[_const_spec(w.shape) for w in weights]],
        out_specs=[blk, blk, blk,
                   pl.BlockSpec((None, TP // T, SUBLANES, T), lambda b, i: (b, i, 0, 0)),
                   pl.BlockSpec((None, TP, LANES), lambda b, i: (b, i, 0))],
        out_shape=[big, big, big,
                   jax.ShapeDtypeStruct((nb, L // T, SUBLANES, T), F32),
                   jax.ShapeDtypeStruct((nb, L, LANES), F32)],
        compiler_params=pltpu.CompilerParams(
            dimension_semantics=("arbitrary", "arbitrary"),
            vmem_limit_bytes=VMEM_LIMIT_BYTES),
        name="mlstm_prep",
    )(xm, xc, *weights)


def _mlstm_kernel(q_ref, k_ref, v_ref, xc_ref, zs_ref, row_ref, col_ref, skip_ref, gn_ref,
                  c0_ref, n0_ref, m0_ref, h_ref, *rest, T, emit_state):
    if emit_state:
        co_ref, no_ref, mo_ref, c_s, cb_s, n_s, m_s = rest
    else:
        c_s, cb_s, n_s, m_s = rest
    ci = pl.program_id(1)
    nc = pl.num_programs(1)

    @pl.when(ci == 0)
    def _():
        c_s[...] = c0_ref[...]
        cb_s[...] = c0_ref[...].astype(BF16)
        n_s[...] = n0_ref[...]
        m_s[...] = m0_ref[...]

    rowp = row_ref[...]
    colp = col_ref[...]
    causal = (lax.broadcasted_iota(jnp.int32, (T, T), 0)
              >= lax.broadcasted_iota(jnp.int32, (T, T), 1))

    heads = range(M_HEADS)
    sls = [slice(h * M_HEAD_DIM, (h + 1) * M_HEAD_DIM) for h in heads]

    s_raw = [lax.dot_general(q_ref[:, sls[h]], k_ref[:, sls[h]], _NT, preferred_element_type=F32)
             for h in heads]
    qc = [jnp.dot(q_ref[:, sls[h]], cb_s[h], preferred_element_type=F32) for h in heads]

    dmat, w_inter, e_negm, w_k, decay, m_new = [], [], [], [], [], []
    for h in heads:
        i_row = rowp[h:h + 1, :]
        f_row = rowp[M_HEADS + h:M_HEADS + h + 1, :]
        i_col = colp[:, h:h + 1]
        f_col = colp[:, M_HEADS + h:M_HEADS + h + 1]
        g_tot = f_row[:, T - 1:T]
        m_prev = m_s[h][0:1, 0:1]
        b_row = i_row - f_row
        log_d = jnp.where(causal, f_col + b_row, NEG)
        inter = f_col + m_prev
        m_row = jnp.maximum(inter, jnp.max(log_d, axis=1, keepdims=True))
        dmat.append(jnp.exp(log_d - m_row))
        w_inter.append(jnp.exp(inter - m_row))
        e_negm.append(jnp.exp(-m_row))
        mn = jnp.maximum(g_tot + m_prev, jnp.max(g_tot + b_row, axis=1, keepdims=True))
        m_new.append(mn)
        w_k.append(jnp.exp(g_tot - f_col + i_col - mn))
        decay.append(jnp.exp(g_tot + m_prev - mn))

    hhs = []
    for h in heads:
        qn = jnp.sum(q_ref[:, sls[h]].astype(F32) * n_s[h][0:1, :], axis=1, keepdims=True)
        s = s_raw[h] * dmat[h]
        num = (jnp.dot(s.astype(BF16), v_ref[:, sls[h]], preferred_element_type=F32)
               + w_inter[h] * qc[h])
        den = jnp.sum(s, axis=1, keepdims=True) + w_inter[h] * qn
        hhs.append(num * (1.0 / jnp.maximum(jnp.abs(den), e_negm[h])))
    for h in heads:
        kw = k_ref[:, sls[h]].astype(F32) * w_k[h]
        c_new = decay[h] * c_s[h] + lax.dot_general(kw.astype(BF16), v_ref[:, sls[h]], _TN,
                                                    preferred_element_type=F32)
        n_new = decay[h] * n_s[h][0:1, :] + jnp.sum(kw, axis=0, keepdims=True)
        c_s[h] = c_new
        cb_s[h] = c_new.astype(BF16)
        n_s[h] = jnp.broadcast_to(n_new, (SUBLANES, M_HEAD_DIM))
        m_s[h] = jnp.broadcast_to(m_new[h], (SUBLANES, LANES))
    for h in heads:
        hn = _head_norm(hhs[h], gn_ref[:, sls[h]])
        out = ((hn + skip_ref[:, sls[h]] * xc_ref[:, sls[h]].astype(F32))
               * zs_ref[:, sls[h]].astype(F32))
        h_ref[:, sls[h]] = out.astype(BF16)

    if emit_state:
        @pl.when(ci == nc - 1)
        def _():
            co_ref[...] = c_s[...]
            no_ref[...] = n_s[...]
            mo_ref[...] = m_s[...]


def _mlstm(q, k, v, xc, zs, rowp, colp, skip, gn, state, T, emit_state):
    nb, L, _ = q.shape
    nc = L // T
    blk = pl.BlockSpec((None, T, M_INNER), lambda b, c: (b, c, 0))
    in_specs = [blk, blk, blk, blk, blk,
                pl.BlockSpec((None, None, SUBLANES, T), lambda b, c: (b, c, 0, 0)),
                pl.BlockSpec((None, T, LANES), lambda b, c: (b, c, 0)),
                _const_spec(skip.shape), _const_spec(gn.shape),
                *[_const_spec(s.shape) for s in state]]
    out_shape = [jax.ShapeDtypeStruct((nb, L, M_INNER), BF16)]
    out_specs = [blk]
    if emit_state:
        assert nb == 1
        for s in state:
            out_shape.append(jax.ShapeDtypeStruct(s.shape, F32))
            out_specs.append(pl.BlockSpec(s.shape, lambda b, c, _n=len(s.shape): (0,) * _n))
    c_shape = state[0].shape
    scratch = [pltpu.VMEM(c_shape, F32), pltpu.VMEM(c_shape, BF16),
               pltpu.VMEM(state[1].shape, F32), pltpu.VMEM(state[2].shape, F32)]
    return pl.pallas_call(
        functools.partial(_mlstm_kernel, T=T, emit_state=emit_state),
        grid=(nb, nc),
        in_specs=in_specs,
        out_specs=out_specs,
        out_shape=out_shape,
        scratch_shapes=scratch,
        compiler_params=pltpu.CompilerParams(
            dimension_semantics=("arbitrary", "arbitrary"),
            vmem_limit_bytes=VMEM_LIMIT_BYTES),
        name="mlstm_state" if emit_state else "mlstm",
    )(q, k, v, xc, zs, rowp, colp, skip, gn, *state)


def _retention_kernel(q_ref, k_ref, v_ref, gs_ref, gn_ref, s0_ref,
                      o_ref, *rest, T, emit_state):
    if emit_state:
        so_ref, st_s, dm_s = rest
    else:
        st_s, dm_s = rest
    ci = pl.program_id(1)
    nc = pl.num_programs(1)
    log_gamma = [math.log(1.0 - 2.0 ** (-5.0 - h)) for h in range(R_HEADS)]

    @pl.when(ci == 0)
    def _():
        st_s[...] = s0_ref[...]
        diff = (lax.broadcasted_iota(jnp.int32, (T, T), 0)
                - lax.broadcasted_iota(jnp.int32, (T, T), 1))
        dfl = diff.astype(F32)
        for h in range(R_HEADS):
            dm_s[h] = jnp.where(diff >= 0, jnp.exp(log_gamma[h] * dfl), 0.0)

    tf = lax.broadcasted_iota(jnp.int32, (T, 1), 0).astype(F32)
    for h in range(R_HEADS):
        qh = q_ref[:, h * R_QK_DIM:(h + 1) * R_QK_DIM]
        kh = k_ref[:, h * R_QK_DIM:(h + 1) * R_QK_DIM]
        sl = slice(h * R_V_DIM, (h + 1) * R_V_DIM)
        vh = v_ref[:, sl]
        s = lax.dot_general(qh, kh, _NT, preferred_element_type=F32) * dm_s[h]
        q_decay = jnp.exp(log_gamma[h] * (tf + 1.0))
        k_decay = jnp.exp(log_gamma[h] * (T - 1.0 - tf))
        o = (jnp.dot(s.astype(BF16), vh, preferred_element_type=F32)
             + q_decay * jnp.dot(qh, st_s[h].astype(BF16), preferred_element_type=F32))
        kd = (kh.astype(F32) * k_decay).astype(BF16)
        st_s[h] = (math.exp(log_gamma[h] * T) * st_s[h]
                   + lax.dot_general(kd, vh, _TN, preferred_element_type=F32))
        o_ref[:, sl] = (_head_norm(o, gn_ref[:, sl]) * gs_ref[:, sl].astype(F32)).astype(BF16)

    if emit_state:
        @pl.when(ci == nc - 1)
        def _():
            so_ref[...] = st_s[...]


def _retention(q, k, v, gs, gn, s0, T, emit_state):
    nb, L, _ = q.shape
    nc = L // T
    qk_blk = pl.BlockSpec((None, T, R_QK), lambda b, c: (b, c, 0))
    v_blk = pl.BlockSpec((None, T, R_V), lambda b, c: (b, c, 0))
    in_specs = [qk_blk, qk_blk, v_blk, v_blk, _const_spec(gn.shape), _const_spec(s0.shape)]
    out_shape = [jax.ShapeDtypeStruct((nb, L, R_V), BF16)]
    out_specs = [v_blk]
    if emit_state:
        assert nb == 1
        out_shape.append(jax.ShapeDtypeStruct(s0.shape, F32))
        out_specs.append(pl.BlockSpec(s0.shape, lambda b, c: (0, 0, 0)))
    scratch = [pltpu.VMEM(s0.shape, F32), pltpu.VMEM((R_HEADS, T, T), F32)]
    return pl.pallas_call(
        functools.partial(_retention_kernel, T=T, emit_state=emit_state),
        grid=(nb, nc),
        in_specs=in_specs,
        out_specs=out_specs,
        out_shape=out_shape,
        scratch_shapes=scratch,
        compiler_params=pltpu.CompilerParams(
            dimension_semantics=("arbitrary", "arbitrary"),
            vmem_limit_bytes=VMEM_LIMIT_BYTES),
        name="retention_state" if emit_state else "retention",
    )(q, k, v, gs, gn, s0)


def _rms(x, g):
    return x * lax.rsqrt(jnp.mean(x * x, axis=-1, keepdims=True) + EPS) * g


def _out_ffn_kernel(hm_ref, hr_ref, ga_ref, gb_ref, x_ref, wpm_ref, wpr_ref, wout_ref,
                    wup_ref, wdn_ref, g1_ref, g2_ref, g3_ref, o_ref):
    ya = jnp.dot(hm_ref[...], wpm_ref[...], preferred_element_type=F32)
    yb = jnp.dot(hr_ref[...], wpr_ref[...], preferred_element_type=F32)
    mixin = ga_ref[...].astype(F32) * ya + gb_ref[...].astype(F32) * yb
    mix = jnp.dot(mixin.astype(BF16), wout_ref[...], preferred_element_type=F32)
    h1 = x_ref[...] + _rms(mix, g1_ref[...])
    u = _rms(h1, g2_ref[...]).astype(BF16)
    f = None
    for j in range(D_FF // FF_TILE):
        sl = slice(j * FF_TILE, (j + 1) * FF_TILE)
        a = jnp.maximum(jnp.dot(u, wup_ref[:, sl], preferred_element_type=F32), 0.0)
        part = jnp.dot((a * a).astype(BF16), wdn_ref[sl, :], preferred_element_type=F32)
        f = part if f is None else f + part
    o_ref[...] = h1 + _rms(f, g3_ref[...])


def _out_ffn(hm, hr, ga, gb, x2d, wpm, wpr, wout, wup, wdn, g1, g2, g3, tm):
    rows = x2d.shape[0]
    row_spec = lambda w: pl.BlockSpec((tm, w), lambda i: (i, 0))
    consts = (wpm, wpr, wout, wup, wdn, g1, g2, g3)
    return pl.pallas_call(
        _out_ffn_kernel,
        grid=(rows // tm,),
        in_specs=[row_spec(M_INNER), row_spec(R_V), row_spec(D_MODEL), row_spec(D_MODEL),
                  row_spec(D_MODEL), *[_const_spec(c.shape) for c in consts]],
        out_specs=row_spec(D_MODEL),
        out_shape=jax.ShapeDtypeStruct((rows, D_MODEL), F32),
        compiler_params=pltpu.CompilerParams(
            dimension_semantics=("arbitrary",),
            vmem_limit_bytes=VMEM_LIMIT_BYTES),
        name="out_ffn",
    )(hm, hr, ga, gb, x2d, *consts)


def _block_diag_tiles(w):
    rows = w.reshape(M_INNER, M_QKV_BLOCK)
    wide = jnp.tile(rows, (1, BD_TILE // M_QKV_BLOCK))
    r_blk = (lax.broadcasted_iota(jnp.int32, wide.shape, 0) % BD_TILE) // M_QKV_BLOCK
    c_blk = lax.broadcasted_iota(jnp.int32, wide.shape, 1) // M_QKV_BLOCK
    t = jnp.where(r_blk == c_blk, wide, 0.0).astype(BF16)
    return t.reshape(M_INNER // BD_TILE, BD_TILE, BD_TILE)


def _rope_tables(pos):
    half = R_QK_DIM // 2
    inv = ROPE_BASE ** (-jnp.arange(half, dtype=F32) / half)
    ang = pos.astype(F32)[:, None] * inv[None, :]
    return jnp.cos(ang), jnp.sin(ang)


def kernel(x, meta_tokens, norm_mix_pre, w_in, conv_w, conv_b, w_q_m, w_k_m, w_v_m, w_if, b_if,
           skip_m, gn_m, gn_r, w_proj_m, w_proj_r, w_out, norm_mix_post, norm_ffn_pre, w_up,
           w_down, norm_ffn_post):
    nb, seq, _ = x.shape
    assert norm_mix_pre.shape[0] == 1, "single-layer block"
    assert seq % PROJ_TM_WIDE == 0 and seq % CHUNK_T == 0
    row = lambda a: a[0].reshape(1, -1).astype(F32)
    n_pad = META_T - N_META

    w_in_bf = w_in[0].astype(BF16)
    g_pre = row(norm_mix_pre)
    cw, cb = conv_w[0].astype(F32), row(conv_b)

    meta_rows = jnp.concatenate(
        [jnp.zeros((n_pad, D_MODEL), x.dtype), meta_tokens.astype(x.dtype)], axis=0)
    cos_m, sin_m = _rope_tables(jnp.arange(META_T, dtype=jnp.int32) - n_pad)
    cos_x, sin_x = _rope_tables(jnp.arange(seq, dtype=jnp.int32) + N_META)
    tail_zero = jnp.zeros((SUBLANES, M_INNER), F32)
    m_meta, r_meta, _ = _in_proj_all(meta_rows, g_pre, w_in_bf, cw, cb, tail_zero, cos_m, sin_m,
                                     META_T, 1, n_pad)
    x2d = x.reshape(nb * seq, D_MODEL)
    m_in, r_in, (ga, gb) = _in_proj_all(x2d, g_pre, w_in_bf, cw, cb, m_meta[3], cos_x, sin_x,
                                        PROJ_TM, seq // PROJ_TM, 0)

    nblk = M_INNER // M_QKV_BLOCK
    wif3 = w_if[0].reshape(3, nblk, M_QKV_BLOCK, 2 * M_HEADS)
    fold = lambda w, part: jnp.einsum('nio,noj->nij', w, part).reshape(M_INNER, 2 * M_HEADS)
    wg = jnp.concatenate([fold(w_q_m[0], wif3[0]) + fold(w_k_m[0], wif3[1]), fold(w_v_m[0], wif3[2])],
                         axis=0)
    wif = jnp.zeros((2 * M_INNER, GATE_PAD), F32).at[:, :2 * M_HEADS].set(wg).astype(BF16)
    bif = jnp.zeros((1, LANES), F32).at[0, :2 * M_HEADS].set(b_if[0])
    p_weights = (_block_diag_tiles(w_q_m[0]), _block_diag_tiles(w_k_m[0]),
                 _block_diag_tiles(w_v_m[0]), wif, bif)
    m_zero = (jnp.zeros((M_HEADS, M_HEAD_DIM, M_HEAD_DIM), F32),
              jnp.zeros((M_HEADS, SUBLANES, M_HEAD_DIM), F32),
              jnp.zeros((M_HEADS, SUBLANES, LANES), F32))
    as_seq = lambda a, n, t: a.reshape(n, t, a.shape[-1])
    xm_m, xc_m, zs_m = [as_seq(a, 1, META_T) for a in m_meta[:3]]
    xm_x, xc_x, zs_x = [as_seq(a, nb, seq) for a in m_in[:3]]
    prep_m = _mlstm_prep(xm_m, xc_m, p_weights, META_T, META_T, n_pad)
    m_init = _mlstm(*prep_m[:3], xc_m, zs_m, *prep_m[3:], row(skip_m), row(gn_m), m_zero,
                    META_T, True)[1:]
    prep_x = _mlstm_prep(xm_x, xc_x, p_weights, PREP_TP, CHUNK_T, 0)
    hm = _mlstm(*prep_x[:3], xc_x, zs_x, *prep_x[3:], row(skip_m), row(gn_m), tuple(m_init),
                CHUNK_T, False)[0]

    r_zero = jnp.zeros((R_HEADS, R_QK_DIM, R_V_DIM), F32)
    r_init = _retention(*[as_seq(a, 1, META_T) for a in r_meta], row(gn_r), r_zero,
                        META_T, True)[1]
    hr = _retention(*[as_seq(a, nb, seq) for a in r_in], row(gn_r), r_init, CHUNK_T, False)[0]

    out = _out_ffn(hm.reshape(nb * seq, M_INNER), hr.reshape(nb * seq, R_V), ga, gb, x2d,
                   w_proj_m[0].astype(BF16), w_proj_r[0].astype(BF16), w_out[0].astype(BF16),
                   w_up[0].astype(BF16), w_down[0].astype(BF16),
                   row(norm_mix_post), row(norm_ffn_pre), row(norm_ffn_post), OUT_TM)
    return out.reshape(nb, seq, D_MODEL)
```

```python
import functools
import math

import jax
import jax.numpy as jnp
import numpy as np
from jax import lax
from jax.experimental import pallas as pl
from jax.experimental.pallas import tpu as pltpu

F32 = jnp.float32
BF16 = jnp.bfloat16

D_MODEL = 1024
N_META = 16
M_INNER = 2 * D_MODEL
M_HEADS = 4
M_HEAD_DIM = M_INNER // M_HEADS
M_QKV_BLOCK = 4
M_CONV = 4
R_HEADS = 4
R_QK_DIM = D_MODEL // R_HEADS
R_V_DIM = 2 * R_QK_DIM
R_QK = R_HEADS * R_QK_DIM
R_V = R_HEADS * R_V_DIM
D_FF = 4 * D_MODEL
ROPE_BASE = 10000.0
EPS = 1e-6
NEG = -1e30
LOG2E = math.log2(math.e)
N_IN = 2 * M_INNER + 2 * R_QK + 2 * R_V + 2 * D_MODEL

LANES = 128
SUBLANES = 8
MXU_DIM = 256
VMEM_LIMIT_BYTES = 56 * 1024 * 1024

CHUNK_T = 256
META_T = 128
BD_TILE = MXU_DIM
GATE_PAD = MXU_DIM
PREP_TP = 512
CONV_STRIP = 256
PROJ_TM = 512
PROJ_TM_WIDE = 1024
OUT_TM = 512
FF_TILE = 1024

_NT = (((1,), (1,)), ((), ()))
_TN = (((0,), (0,)), ((), ()))


def _const_spec(shape):
    nd = len(shape)
    return pl.BlockSpec(shape, lambda *_: (0,) * nd, pipeline_mode=pl.Buffered(1))


def _sigmoid(x):
    return 1.0 / (1.0 + jnp.exp2(x * (-LOG2E)))


def _silu(x):
    return x * _sigmoid(x)


def _normed(x_ref, g_ref):
    x = x_ref[...]
    ms = jnp.mean(x * x, axis=-1, keepdims=True)
    return (x * lax.rsqrt(ms + EPS) * g_ref[...]).astype(BF16)


def _inproj_mlstm_kernel(x_ref, g_ref, w_ref, cw_ref, cb_ref, t0_ref,
                         xm_ref, xc_ref, zs_ref, to_ref, halo_s, res_s, *, tm, tiles_per_seq):
    i = pl.program_id(0)

    @pl.when(i % tiles_per_seq == 0)
    def _():
        halo_s[...] = t0_ref[...]

    u = _normed(x_ref, g_ref)
    nstrip = M_INNER // CONV_STRIP
    nv = tm // SUBLANES
    sub = lax.broadcasted_iota(jnp.int32, (nv, SUBLANES, CONV_STRIP), 1)

    def strip_dot(c):
        sl = slice(c * CONV_STRIP, (c + 1) * CONV_STRIP)
        res_s[(i + c) % 2] = jnp.dot(u, w_ref[:, sl], preferred_element_type=F32)

    strip_dot(0)
    for c in range(nstrip):
        sl = slice(c * CONV_STRIP, (c + 1) * CONV_STRIP)
        if c + 1 < nstrip:
            strip_dot(c + 1)
        zsl = slice(M_INNER + c * CONV_STRIP, M_INNER + (c + 1) * CONV_STRIP)
        zm = jnp.dot(u, w_ref[:, zsl], preferred_element_type=F32)
        zs_ref[:, sl] = _silu(zm).astype(BF16)
        xm = res_s[(i + c) % 2]
        xm_ref[:, sl] = xm.astype(BF16)
        x3 = xm.reshape(nv, SUBLANES, CONV_STRIP)
        prev3 = jnp.concatenate([halo_s[:, sl].reshape(1, SUBLANES, CONV_STRIP), x3[:nv - 1]], axis=0)
        y = cb_ref[:, sl] + cw_ref[M_CONV - 1:M_CONV, sl] * x3
        for j in range(M_CONV - 1):
            sh = M_CONV - 1 - j
            merged = jnp.where(sub >= SUBLANES - sh, prev3, x3)
            y = y + cw_ref[j:j + 1, sl] * pltpu.roll(merged, sh, axis=1)
        halo_s[:, sl] = xm[tm - SUBLANES:, :]
        xc_ref[:, sl] = _silu(y.reshape(tm, CONV_STRIP)).astype(BF16)

    @pl.when(i == pl.num_programs(0) - 1)
    def _():
        to_ref[...] = halo_s[...]


def _inproj_rot_kernel(x_ref, g_ref, w_ref, cos_ref, sin_ref, q_ref, k_ref, *, tm, n_pad):
    u = _normed(x_ref, g_ref)
    cos = cos_ref[...]
    sin = sin_ref[...]
    half = R_QK_DIM // 2
    kscale = R_QK_DIM ** -0.5
    if n_pad:
        kmul = (lax.broadcasted_iota(jnp.int32, (tm, 1), 0) >= n_pad).astype(F32) * kscale

    def rot(t, h):
        t1 = t[:, h * R_QK_DIM:h * R_QK_DIM + half]
        t2 = t[:, h * R_QK_DIM + half:(h + 1) * R_QK_DIM]
        return t1 * cos - t2 * sin, t1 * sin + t2 * cos

    q = jnp.dot(u, w_ref[:, 0:R_QK], preferred_element_type=F32)
    for h in range(R_HEADS):
        a, b = rot(q, h)
        q_ref[:, h * R_QK_DIM:h * R_QK_DIM + half] = a.astype(BF16)
        q_ref[:, h * R_QK_DIM + half:(h + 1) * R_QK_DIM] = b.astype(BF16)
    k = jnp.dot(u, w_ref[:, R_QK:2 * R_QK], preferred_element_type=F32)
    for h in range(R_HEADS):
        a, b = rot(k, h)
        if n_pad:
            a, b = a * kmul, b * kmul
        else:
            a, b = a * kscale, b * kscale
        k_ref[:, h * R_QK_DIM:h * R_QK_DIM + half] = a.astype(BF16)
        k_ref[:, h * R_QK_DIM + half:(h + 1) * R_QK_DIM] = b.astype(BF16)


def _inproj_vg_kernel(x_ref, g_ref, w_ref, v_ref, gs_ref):
    u = _normed(x_ref, g_ref)
    v_ref[...] = jnp.dot(u, w_ref[:, 0:R_V], preferred_element_type=F32).astype(BF16)
    gs_ref[...] = _silu(jnp.dot(u, w_ref[:, R_V:2 * R_V], preferred_element_type=F32)).astype(BF16)


def _inproj_gate_kernel(x_ref, g_ref, w_ref, ga_ref, gb_ref):
    u = _normed(x_ref, g_ref)
    ga_ref[...] = _sigmoid(jnp.dot(u, w_ref[:, 0:D_MODEL], preferred_element_type=F32)).astype(BF16)
    gb_ref[...] = _sigmoid(
        jnp.dot(u, w_ref[:, D_MODEL:2 * D_MODEL], preferred_element_type=F32)).astype(BF16)


def _in_proj_call(body, name, x2d, gain, w_grp, extra_in, extra_specs, out_widths, tm,
                  extra_out=(), extra_out_specs=(), scratch=()):
    rows = x2d.shape[0]
    row_spec = lambda w: pl.BlockSpec((tm, w), lambda i: (i, 0))
    return pl.pallas_call(
        body,
        grid=(rows // tm,),
        in_specs=[row_spec(D_MODEL), _const_spec(gain.shape), _const_spec(w_grp.shape),
                  *extra_specs],
        out_specs=[*[row_spec(w) for w in out_widths], *extra_out_specs],
        out_shape=[*[jax.ShapeDtypeStruct((rows, w), BF16) for w in out_widths], *extra_out],
        scratch_shapes=list(scratch),
        compiler_params=pltpu.CompilerParams(
            dimension_semantics=("arbitrary",),
            vmem_limit_bytes=VMEM_LIMIT_BYTES),
        name=name,
    )(x2d, gain, w_grp, *extra_in)


def _split_w_in(w):
    groups, o = [], 0
    for width in (2 * M_INNER, 2 * R_QK, 2 * R_V, 2 * D_MODEL):
        groups.append(w[:, o:o + width].astype(BF16))
        o += width
    return tuple(groups)


def _in_proj_all(x2d, gain, w_groups, cw, cb, tail0, cos, sin, tm, tiles_per_seq, n_pad):
    w_m, w_qk, w_vg, w_g = w_groups
    tail_shape = (SUBLANES, M_INNER)
    xm, xc, zs, tail = _in_proj_call(
        functools.partial(_inproj_mlstm_kernel, tm=tm, tiles_per_seq=tiles_per_seq),
        "in_proj_mlstm", x2d, gain, w_m, (cw, cb, tail0),
        (_const_spec(cw.shape), _const_spec(cb.shape), _const_spec(tail_shape)),
        (M_INNER, M_INNER, M_INNER), tm,
        extra_out=(jax.ShapeDtypeStruct(tail_shape, F32),),
        extra_out_specs=(pl.BlockSpec(tail_shape, lambda i: (0, 0)),),
        scratch=(pltpu.VMEM(tail_shape, F32), pltpu.VMEM((2, tm, CONV_STRIP), F32)))
    half = R_QK_DIM // 2
    tmw = min(tm * PROJ_TM_WIDE // PROJ_TM, x2d.shape[0])
    tps_w = tiles_per_seq * tm // tmw
    pos_spec = pl.BlockSpec((tmw, half), lambda i: (i % tps_w, 0))
    qr, kr = _in_proj_call(
        functools.partial(_inproj_rot_kernel, tm=tmw, n_pad=n_pad),
        "in_proj_rot", x2d, gain, w_qk, (cos, sin), (pos_spec, pos_spec), (R_QK, R_QK), tmw)
    v, gs = _in_proj_call(_inproj_vg_kernel, "in_proj_vg", x2d, gain, w_vg, (), (), (R_V, R_V), tmw)
    ga, gb = _in_proj_call(_inproj_gate_kernel, "in_proj_gate", x2d, gain, w_g, (), (),
                           (D_MODEL, D_MODEL), tmw)
    return (xm, xc, zs, tail), (qr, kr, v, gs), (ga, gb)


def _head_norm(h, w):
    mu = jnp.mean(h, axis=-1, keepdims=True)
    c = h - mu
    var = jnp.mean(c * c, axis=-1, keepdims=True)
    return c * lax.rsqrt(var + EPS) * w


def _lane_cumsum(x, n):
    lane = lax.broadcasted_iota(jnp.int32, x.shape, 1)
    d = 1
    while d < n:
        x = x + jnp.where(lane >= d, pltpu.roll(x, d, axis=1), 0.0)
        d *= 2
    return x


def _mlstm_prep_kernel(xm_ref, xc_ref, bdq_ref, bdk_ref, bdv_ref, wif_ref, bif_ref,
                       q_ref, k_ref, v_ref, row_ref, col_ref, *, TP, T, n_pad):
    scale = M_HEAD_DIM ** -0.5
    for j in range(M_INNER // BD_TILE):
        sl = slice(j * BD_TILE, (j + 1) * BD_TILE)
        q_ref[:, sl] = jnp.dot(xc_ref[:, sl], bdq_ref[j], preferred_element_type=F32).astype(BF16)
        kj = jnp.dot(xc_ref[:, sl], bdk_ref[j], preferred_element_type=F32)
        k_ref[:, sl] = (kj * scale).astype(BF16)
        v_ref[:, sl] = jnp.dot(xm_ref[:, sl], bdv_ref[j], preferred_element_type=F32).astype(BF16)
    xcm = jnp.concatenate([xc_ref[...], xm_ref[...]], axis=1)
    g_all = jnp.dot(xcm, wif_ref[...], preferred_element_type=F32)[:, 0:LANES] + bif_ref[...]
    lane = lax.broadcasted_iota(jnp.int32, (T, LANES), 1)
    subl = lax.broadcasted_iota(jnp.int32, (SUBLANES, T), 0)
    for c in range(TP // T):
        g = g_all[c * T:(c + 1) * T, :]
        if n_pad:
            rowv = lax.broadcasted_iota(jnp.int32, (T, LANES), 0) >= n_pad
            g = jnp.where(jnp.logical_and(lane < M_HEADS, jnp.logical_not(rowv)), NEG, g)
        gt = g.T
        lf_t = jnp.minimum(gt, 0.0) - jnp.log1p(jnp.exp(-jnp.abs(gt)))
        cum_t = _lane_cumsum(lf_t, T)
        row_ref[c] = jnp.where(subl < M_HEADS, gt[0:SUBLANES, :], cum_t[0:SUBLANES, :])
        col_ref[c * T:(c + 1) * T, :] = jnp.where(lane < M_HEADS, g, cum_t.T)


def _mlstm_prep(xm, xc, weights, TP, T, n_pad):
    nb, L, _ = xm.shape
    blk = pl.BlockSpec((None, TP, M_INNER), lambda b, i: (b, i, 0))
    big = jax.ShapeDtypeStruct((nb, L, M_INNER), BF16)
    return pl.pallas_call(
        functools.partial(_mlstm_prep_kernel, TP=TP, T=T, n_pad=n_pad),
        grid=(nb, L // TP),
        in_specs=[blk, blk, *[_const_spec(w.shape) for w in weights]],
        out_specs=[blk, blk, blk,
                   pl.BlockSpec((None, TP // T, SUBLANES, T), lambda b, i: (b, i, 0, 0)),
                   pl.BlockSpec((None, TP, LANES), lambda b, i: (b, i, 0))],
        out_shape=[big, big, big,
                   jax.ShapeDtypeStruct((nb, L // T, SUBLANES, T), F32),
                   jax.ShapeDtypeStruct((nb, L, LANES), F32)],
        compiler_params=pltpu.CompilerParams(
            dimension_semantics=("arbitrary", "arbitrary"),
            vmem_limit_bytes=VMEM_LIMIT_BYTES),
        name="mlstm_prep",
    )(xm, xc, *weights)


def _mlstm_kernel(q_ref, k_ref, v_ref, xc_ref, zs_ref, row_ref, col_ref, skip_ref, gn_ref,
                  c0_ref, n0_ref, m0_ref, h_ref, *rest, T, emit_state):
    if emit_state:
        co_ref, no_ref, mo_ref, c_s, cb_s, n_s, m_s = rest
    else:
        c_s, cb_s, n_s, m_s = rest
    ci = pl.program_id(1)
    nc = pl.num_programs(1)

    @pl.when(ci == 0)
    def _():
        c_s[...] = c0_ref[...]
        cb_s[...] = c0_ref[...].astype(BF16)
        n_s[...] = n0_ref[...]
        m_s[...] = m0_ref[...]

    rowp = row_ref[...]
    colp = col_ref[...]
    causal = (lax.broadcasted_iota(jnp.int32, (T, T), 0)
              >= lax.broadcasted_iota(jnp.int32, (T, T), 1))

    heads = range(M_HEADS)
    sls = [slice(h * M_HEAD_DIM, (h + 1) * M_HEAD_DIM) for h in heads]

    s_raw = [lax.dot_general(q_ref[:, sls[h]], k_ref[:, sls[h]], _NT, preferred_element_type=F32)
             for h in heads]
    qc = [jnp.dot(q_ref[:, sls[h]], cb_s[h], preferred_element_type=F32) for h in heads]

    dmat, w_inter, e_negm, w_k, decay, m_new = [], [], [], [], [], []
    for h in heads:
        i_row = rowp[h:h + 1, :]
        f_row = rowp[M_HEADS + h:M_HEADS + h + 1, :]
        i_col = colp[:, h:h + 1]
        f_col = colp[:, M_HEADS + h:M_HEADS + h + 1]
        g_tot = f_row[:, T - 1:T]
        m_prev = m_s[h][0:1, 0:1]
        b_row = i_row - f_row
        log_d = jnp.where(causal, f_col + b_row, NEG)
        inter = f_col + m_prev
        m_row = jnp.maximum(inter, jnp.max(log_d, axis=1, keepdims=True))
        dmat.append(jnp.exp(log_d - m_row))
        w_inter.append(jnp.exp(inter - m_row))
        e_negm.append(jnp.exp(-m_row))
        mn = jnp.maximum(g_tot + m_prev, jnp.max(g_tot + b_row, axis=1, keepdims=True))
        m_new.append(mn)
        w_k.append(jnp.exp(g_tot - f_col + i_col - mn))
        decay.append(jnp.exp(g_tot + m_prev - mn))

    hhs = []
    for h in heads:
        qn = jnp.sum(q_ref[:, sls[h]].astype(F32) * n_s[h][0:1, :], axis=1, keepdims=True)
        s = s_raw[h] * dmat[h]
        num = (jnp.dot(s.astype(BF16), v_ref[:, sls[h]], preferred_element_type=F32)
               + w_inter[h] * qc[h])
        den = jnp.sum(s, axis=1, keepdims=True) + w_inter[h] * qn
        hhs.append(num * (1.0 / jnp.maximum(jnp.abs(den), e_negm[h])))
    for h in heads:
        kw = k_ref[:, sls[h]].astype(F32) * w_k[h]
        c_new = decay[h] * c_s[h] + lax.dot_general(kw.astype(BF16), v_ref[:, sls[h]], _TN,
                                                    preferred_element_type=F32)
        n_new = decay[h] * n_s[h][0:1, :] + jnp.sum(kw, axis=0, keepdims=True)
        c_s[h] = c_new
        cb_s[h] = c_new.astype(BF16)
        n_s[h] = jnp.broadcast_to(n_new, (SUBLANES, M_HEAD_DIM))
        m_s[h] = jnp.broadcast_to(m_new[h], (SUBLANES, LANES))
    for h in heads:
        hn = _head_norm(hhs[h], gn_ref[:, sls[h]])
        out = ((hn + skip_ref[:, sls[h]] * xc_ref[:, sls[h]].astype(F32))
               * zs_ref[:, sls[h]].astype(F32))
        h_ref[:, sls[h]] = out.astype(BF16)

    if emit_state:
        @pl.when(ci == nc - 1)
        def _():
            co_ref[...] = c_s[...]
            no_ref[...] = n_s[...]
            mo_ref[...] = m_s[...]


def _mlstm(q, k, v, xc, zs, rowp, colp, skip, gn, state, T, emit_state):
    nb, L, _ = q.shape
    nc = L // T
    blk = pl.BlockSpec((None, T, M_INNER), lambda b, c: (b, c, 0))
    in_specs = [blk, blk, blk, blk, blk,
                pl.BlockSpec((None, None, SUBLANES, T), lambda b, c: (b, c, 0, 0)),
                pl.BlockSpec((None, T, LANES), lambda b, c: (b, c, 0)),
                _const_spec(skip.shape), _const_spec(gn.shape),
                *[_const_spec(s.shape) for s in state]]
    out_shape = [jax.ShapeDtypeStruct((nb, L, M_INNER), BF16)]
    out_specs = [blk]
    if emit_state:
        assert nb == 1
        for s in state:
            out_shape.append(jax.ShapeDtypeStruct(s.shape, F32))
            out_specs.append(pl.BlockSpec(s.shape, lambda b, c, _n=len(s.shape): (0,) * _n))
    c_shape = state[0].shape
    scratch = [pltpu.VMEM(c_shape, F32), pltpu.VMEM(c_shape, BF16),
               pltpu.VMEM(state[1].shape, F32), pltpu.VMEM(state[2].shape, F32)]
    return pl.pallas_call(
        functools.partial(_mlstm_kernel, T=T, emit_state=emit_state),
        grid=(nb, nc),
        in_specs=in_specs,
        out_specs=out_specs,
        out_shape=out_shape,
        scratch_shapes=scratch,
        compiler_params=pltpu.CompilerParams(
            dimension_semantics=("arbitrary", "arbitrary"),
            vmem_limit_bytes=VMEM_LIMIT_BYTES),
        name="mlstm_state" if emit_state else "mlstm",
    )(q, k, v, xc, zs, rowp, colp, skip, gn, *state)


def _retention_kernel(q_ref, k_ref, v_ref, gs_ref, gn_ref, s0_ref,
                      o_ref, *rest, T, emit_state):
    if emit_state:
        so_ref, st_s, dm_s = rest
    else:
        st_s, dm_s = rest
    ci = pl.program_id(1)
    nc = pl.num_programs(1)
    log_gamma = [math.log(1.0 - 2.0 ** (-5.0 - h)) for h in range(R_HEADS)]

    @pl.when(ci == 0)
    def _():
        st_s[...] = s0_ref[...]
        diff = (lax.broadcasted_iota(jnp.int32, (T, T), 0)
                - lax.broadcasted_iota(jnp.int32, (T, T), 1))
        dfl = diff.astype(F32)
        for h in range(R_HEADS):
            dm_s[h] = jnp.where(diff >= 0, jnp.exp(log_gamma[h] * dfl), 0.0)

    tf = lax.broadcasted_iota(jnp.int32, (T, 1), 0).astype(F32)
    for h in range(R_HEADS):
        qh = q_ref[:, h * R_QK_DIM:(h + 1) * R_QK_DIM]
        kh = k_ref[:, h * R_QK_DIM:(h + 1) * R_QK_DIM]
        sl = slice(h * R_V_DIM, (h + 1) * R_V_DIM)
        vh = v_ref[:, sl]
        s = lax.dot_general(qh, kh, _NT, preferred_element_type=F32) * dm_s[h]
        q_decay = jnp.exp(log_gamma[h] * (tf + 1.0))
        k_decay = jnp.exp(log_gamma[h] * (T - 1.0 - tf))
        o = (jnp.dot(s.astype(BF16), vh, preferred_element_type=F32)
             + q_decay * jnp.dot(qh, st_s[h].astype(BF16), preferred_element_type=F32))
        kd = (kh.astype(F32) * k_decay).astype(BF16)
        st_s[h] = (math.exp(log_gamma[h] * T) * st_s[h]
                   + lax.dot_general(kd, vh, _TN, preferred_element_type=F32))
        o_ref[:, sl] = (_head_norm(o, gn_ref[:, sl]) * gs_ref[:, sl].astype(F32)).astype(BF16)

    if emit_state:
        @pl.when(ci == nc - 1)
        def _():
            so_ref[...] = st_s[...]


def _retention(q, k, v, gs, gn, s0, T, emit_state):
    nb, L, _ = q.shape
    nc = L // T
    qk_blk = pl.BlockSpec((None, T, R_QK), lambda b, c: (b, c, 0))
    v_blk = pl.BlockSpec((None, T, R_V), lambda b, c: (b, c, 0))
    in_specs = [qk_blk, qk_blk, v_blk, v_blk, _const_spec(gn.shape), _const_spec(s0.shape)]
    out_shape = [jax.ShapeDtypeStruct((nb, L, R_V), BF16)]
    out_specs = [v_blk]
    if emit_state:
        assert nb == 1
        out_shape.append(jax.ShapeDtypeStruct(s0.shape, F32))
        out_specs.append(pl.BlockSpec(s0.shape, lambda b, c: (0, 0, 0)))
    scratch = [pltpu.VMEM(s0.shape, F32), pltpu.VMEM((R_HEADS, T, T), F32)]
    return pl.pallas_call(
        functools.partial(_retention_kernel, T=T, emit_state=emit_state),
        grid=(nb, nc),
        in_specs=in_specs,
        out_specs=out_specs,
        out_shape=out_shape,
        scratch_shapes=scratch,
        compiler_params=pltpu.CompilerParams(
            dimension_semantics=("arbitrary", "arbitrary"),
            vmem_limit_bytes=VMEM_LIMIT_BYTES),
        name="retention_state" if emit_state else "retention",
    )(q, k, v, gs, gn, s0)


def _rms(x, g):
    return x * lax.rsqrt(jnp.mean(x * x, axis=-1, keepdims=True) + EPS) * g


def _out_ffn_kernel(hm_ref, hr_ref, ga_ref, gb_ref, x_ref, wpm_ref, wpr_ref, wout_ref,
                    wup_ref, wdn_ref, g1_ref, g2_ref, g3_ref, o_ref):
    ya = jnp.dot(hm_ref[...], wpm_ref[...], preferred_element_type=F32)
    yb = jnp.dot(hr_ref[...], wpr_ref[...], preferred_element_type=F32)
    mixin = ga_ref[...].astype(F32) * ya + gb_ref[...].astype(F32) * yb
    mix = jnp.dot(mixin.astype(BF16), wout_ref[...], preferred_element_type=F32)
    h1 = x_ref[...] + _rms(mix, g1_ref[...])
    u = _rms(h1, g2_ref[...]).astype(BF16)
    f = None
    for j in range(D_FF // FF_TILE):
        sl = slice(j * FF_TILE, (j + 1) * FF_TILE)
        a = jnp.maximum(jnp.dot(u, wup_ref[:, sl], preferred_element_type=F32), 0.0)
        part = jnp.dot((a * a).astype(BF16), wdn_ref[sl, :], preferred_element_type=F32)
        f = part if f is None else f + part
    o_ref[...] = h1 + _rms(f, g3_ref[...])


def _out_ffn(hm, hr, ga, gb, x2d, wpm, wpr, wout, wup, wdn, g1, g2, g3, tm):
    rows = x2d.shape[0]
    row_spec = lambda w: pl.BlockSpec((tm, w), lambda i: (i, 0))
    consts = (wpm, wpr, wout, wup, wdn, g1, g2, g3)
    return pl.pallas_call(
        _out_ffn_kernel,
        grid=(rows // tm,),
        in_specs=[row_spec(M_INNER), row_spec(R_V), row_spec(D_MODEL), row_spec(D_MODEL),
                  row_spec(D_MODEL), *[_const_spec(c.shape) for c in consts]],
        out_specs=row_spec(D_MODEL),
        out_shape=jax.ShapeDtypeStruct((rows, D_MODEL), F32),
        compiler_params=pltpu.CompilerParams(
            dimension_semantics=("arbitrary",),
            vmem_limit_bytes=VMEM_LIMIT_BYTES),
        name="out_ffn",
    )(hm, hr, ga, gb, x2d, *consts)


def _block_diag_tiles(w):
    rows = w.reshape(M_INNER, M_QKV_BLOCK)
    wide = jnp.tile(rows, (1, BD_TILE // M_QKV_BLOCK))
    r_blk = (lax.broadcasted_iota(jnp.int32, wide.shape, 0) % BD_TILE) // M_QKV_BLOCK
    c_blk = lax.broadcasted_iota(jnp.int32, wide.shape, 1) // M_QKV_BLOCK
    t = jnp.where(r_blk == c_blk, wide, 0.0).astype(BF16)
    return t.reshape(M_INNER // BD_TILE, BD_TILE, BD_TILE)


def _rope_tables(pos0, n):
    half = R_QK_DIM // 2
    inv = ROPE_BASE ** (-np.arange(half, dtype=np.float64) / half)
    ang = (pos0 + np.arange(n, dtype=np.float64))[:, None] * inv[None, :]
    return jnp.asarray(np.cos(ang), F32), jnp.asarray(np.sin(ang), F32)


def kernel(x, meta_tokens, norm_mix_pre, w_in, conv_w, conv_b, w_q_m, w_k_m, w_v_m, w_if, b_if,
           skip_m, gn_m, gn_r, w_proj_m, w_proj_r, w_out, norm_mix_post, norm_ffn_pre, w_up,
           w_down, norm_ffn_post):
    nb, seq, _ = x.shape
    assert norm_mix_pre.shape[0] == 1, "single-layer block"
    assert seq % PROJ_TM_WIDE == 0 and seq % CHUNK_T == 0
    row = lambda a: a[0].reshape(1, -1).astype(F32)
    n_pad = META_T - N_META

    w_in_bf = _split_w_in(w_in[0])
    g_pre = row(norm_mix_pre)
    cw, cb = conv_w[0].astype(F32), row(conv_b)

    meta_rows = jnp.concatenate(
        [jnp.zeros((n_pad, D_MODEL), x.dtype), meta_tokens.astype(x.dtype)], axis=0)
    cos_m, sin_m = _rope_tables(-n_pad, META_T)
    cos_x, sin_x = _rope_tables(N_META, seq)
    tail_zero = jnp.zeros((SUBLANES, M_INNER), F32)
    m_meta, r_meta, _ = _in_proj_all(meta_rows, g_pre, w_in_bf, cw, cb, tail_zero, cos_m, sin_m,
                                     META_T, 1, n_pad)
    x2d = x.reshape(nb * seq, D_MODEL)
    m_in, r_in, (ga, gb) = _in_proj_all(x2d, g_pre, w_in_bf, cw, cb, m_meta[3], cos_x, sin_x,
                                        PROJ_TM, seq // PROJ_TM, 0)

    nblk = M_INNER // M_QKV_BLOCK
    wif3 = w_if[0].reshape(3, nblk, M_QKV_BLOCK, 2 * M_HEADS)
    fold = lambda w, part: jnp.einsum('nio,noj->nij', w, part).reshape(M_INNER, 2 * M_HEADS)
    wg = jnp.concatenate([fold(w_q_m[0], wif3[0]) + fold(w_k_m[0], wif3[1]), fold(w_v_m[0], wif3[2])],
                         axis=0)
    wif = jnp.zeros((2 * M_INNER, GATE_PAD), F32).at[:, :2 * M_HEADS].set(wg).astype(BF16)
    bif = jnp.zeros((1, LANES), F32).at[0, :2 * M_HEADS].set(b_if[0])
    p_weights = (_block_diag_tiles(w_q_m[0]), _block_diag_tiles(w_k_m[0]),
                 _block_diag_tiles(w_v_m[0]), wif, bif)
    m_zero = (jnp.zeros((M_HEADS, M_HEAD_DIM, M_HEAD_DIM), F32),
              jnp.zeros((M_HEADS, SUBLANES, M_HEAD_DIM), F32),
              jnp.zeros((M_HEADS, SUBLANES, LANES), F32))
    as_seq = lambda a, n, t: a.reshape(n, t, a.shape[-1])
    xm_m, xc_m, zs_m = [as_seq(a, 1, META_T) for a in m_meta[:3]]
    xm_x, xc_x, zs_x = [as_seq(a, nb, seq) for a in m_in[:3]]
    prep_m = _mlstm_prep(xm_m, xc_m, p_weights, META_T, META_T, n_pad)
    m_init = _mlstm(*prep_m[:3], xc_m, zs_m, *prep_m[3:], row(skip_m), row(gn_m), m_zero,
                    META_T, True)[1:]
    prep_x = _mlstm_prep(xm_x, xc_x, p_weights, PREP_TP, CHUNK_T, 0)
    hm = _mlstm(*prep_x[:3], xc_x, zs_x, *prep_x[3:], row(skip_m), row(gn_m), tuple(m_init),
                CHUNK_T, False)[0]

    r_zero = jnp.zeros((R_HEADS, R_QK_DIM, R_V_DIM), F32)
    r_init = _retention(*[as_seq(a, 1, META_T) for a in r_meta], row(gn_r), r_zero,
                        META_T, True)[1]
    hr = _retention(*[as_seq(a, nb, seq) for a in r_in], row(gn_r), r_init, CHUNK_T, False)[0]

    out = _out_ffn(hm.reshape(nb * seq, M_INNER), hr.reshape(nb * seq, R_V), ga, gb, x2d,
                   w_proj_m[0].astype(BF16), w_proj_r[0].astype(BF16), w_out[0].astype(BF16),
                   w_up[0].astype(BF16), w_down[0].astype(BF16),
                   row(norm_mix_post), row(norm_ffn_pre), row(norm_ffn_post), OUT_TM)
    return out.reshape(nb, seq, D_MODEL)
```

```python
import functools
import math

import jax
import jax.numpy as jnp
import numpy as np
from jax import lax
from jax.experimental import pallas as pl
from jax.experimental.pallas import tpu as pltpu

F32 = jnp.float32
BF16 = jnp.bfloat16

D_MODEL = 1024
N_META = 16
M_INNER = 2 * D_MODEL
M_HEADS = 4
M_HEAD_DIM = M_INNER // M_HEADS
M_QKV_BLOCK = 4
M_CONV = 4
R_HEADS = 4
R_QK_DIM = D_MODEL // R_HEADS
R_V_DIM = 2 * R_QK_DIM
R_QK = R_HEADS * R_QK_DIM
R_V = R_HEADS * R_V_DIM
D_FF = 4 * D_MODEL
ROPE_BASE = 10000.0
EPS = 1e-6
NEG = -1e30
LOG2E = math.log2(math.e)
N_IN = 2 * M_INNER + 2 * R_QK + 2 * R_V + 2 * D_MODEL

LANES = 128
SUBLANES = 8
MXU_DIM = 256
VMEM_LIMIT_BYTES = 56 * 1024 * 1024

CHUNK_T = 256
META_T = 128
BD_TILE = MXU_DIM
GATE_PAD = MXU_DIM
PREP_TP = 512
CONV_STRIP = 256
PROJ_TM = 512
PROJ_TM_WIDE = 1024
OUT_TM = 512
FF_TILE = 1024

_NT = (((1,), (1,)), ((), ()))
_TN = (((0,), (0,)), ((), ()))


def _const_spec(shape):
    nd = len(shape)
    return pl.BlockSpec(shape, lambda *_: (0,) * nd, pipeline_mode=pl.Buffered(1))


def _sigmoid(x):
    return 1.0 / (1.0 + jnp.exp2(x * (-LOG2E)))


def _silu(x):
    return x * _sigmoid(x)


def _normed(x_ref, g_ref):
    x = x_ref[...]
    ms = jnp.mean(x * x, axis=-1, keepdims=True)
    return (x * lax.rsqrt(ms + EPS) * g_ref[...]).astype(BF16)


def _inproj_mlstm_kernel(x_ref, g_ref, w_ref, cw_ref, cb_ref, t0_ref,
                         xm_ref, xc_ref, zs_ref, to_ref, halo_s, res_s, *, tm, tiles_per_seq):
    i = pl.program_id(0)

    @pl.when(i % tiles_per_seq == 0)
    def _():
        halo_s[...] = t0_ref[...]

    u = _normed(x_ref, g_ref)
    nstrip = M_INNER // CONV_STRIP
    nv = tm // SUBLANES
    sub = lax.broadcasted_iota(jnp.int32, (nv, SUBLANES, CONV_STRIP), 1)

    def strip_dot(c):
        sl = slice(c * CONV_STRIP, (c + 1) * CONV_STRIP)
        res_s[(i + c) % 2] = jnp.dot(u, w_ref[:, sl], preferred_element_type=F32)

    strip_dot(0)
    for c in range(nstrip):
        sl = slice(c * CONV_STRIP, (c + 1) * CONV_STRIP)
        if c + 1 < nstrip:
            strip_dot(c + 1)
        zsl = slice(M_INNER + c * CONV_STRIP, M_INNER + (c + 1) * CONV_STRIP)
        zm = jnp.dot(u, w_ref[:, zsl], preferred_element_type=F32)
        zs_ref[:, sl] = _silu(zm).astype(BF16)
        xm = res_s[(i + c) % 2]
        xm_ref[:, sl] = xm.astype(BF16)
        x3 = xm.reshape(nv, SUBLANES, CONV_STRIP)
        prev3 = jnp.concatenate([halo_s[:, sl].reshape(1, SUBLANES, CONV_STRIP), x3[:nv - 1]], axis=0)
        y = cb_ref[:, sl] + cw_ref[M_CONV - 1:M_CONV, sl] * x3
        for j in range(M_CONV - 1):
            sh = M_CONV - 1 - j
            merged = jnp.where(sub >= SUBLANES - sh, prev3, x3)
            y = y + cw_ref[j:j + 1, sl] * pltpu.roll(merged, sh, axis=1)
        halo_s[:, sl] = xm[tm - SUBLANES:, :]
        xc_ref[:, sl] = _silu(y.reshape(tm, CONV_STRIP)).astype(BF16)

    @pl.when(i == pl.num_programs(0) - 1)
    def _():
        to_ref[...] = halo_s[...]


def _inproj_rot_kernel(x_ref, g_ref, w_ref, cos_ref, sin_ref, q_ref, k_ref, *, tm, n_pad):
    u = _normed(x_ref, g_ref)
    cos = cos_ref[...]
    sin = sin_ref[...]
    half = R_QK_DIM // 2
    kscale = R_QK_DIM ** -0.5
    if n_pad:
        kmul = (lax.broadcasted_iota(jnp.int32, (tm, 1), 0) >= n_pad).astype(F32) * kscale

    def rot(t, h):
        t1 = t[:, h * R_QK_DIM:h * R_QK_DIM + half]
        t2 = t[:, h * R_QK_DIM + half:(h + 1) * R_QK_DIM]
        return t1 * cos - t2 * sin, t1 * sin + t2 * cos

    q = jnp.dot(u, w_ref[:, 0:R_QK], preferred_element_type=F32)
    for h in range(R_HEADS):
        a, b = rot(q, h)
        q_ref[:, h * R_QK_DIM:h * R_QK_DIM + half] = a.astype(BF16)
        q_ref[:, h * R_QK_DIM + half:(h + 1) * R_QK_DIM] = b.astype(BF16)
    k = jnp.dot(u, w_ref[:, R_QK:2 * R_QK], preferred_element_type=F32)
    for h in range(R_HEADS):
        a, b = rot(k, h)
        if n_pad:
            a, b = a * kmul, b * kmul
        else:
            a, b = a * kscale, b * kscale
        k_ref[:, h * R_QK_DIM:h * R_QK_DIM + half] = a.astype(BF16)
        k_ref[:, h * R_QK_DIM + half:(h + 1) * R_QK_DIM] = b.astype(BF16)


def _inproj_vg_kernel(x_ref, g_ref, wv_ref, wg_ref, v_ref, gs_ref):
    u = _normed(x_ref, g_ref)
    v_ref[...] = jnp.dot(u, wv_ref[...], preferred_element_type=F32).astype(BF16)
    gs_ref[...] = _silu(jnp.dot(u, wg_ref[...], preferred_element_type=F32)).astype(BF16)


def _inproj_gate_kernel(x_ref, g_ref, w_ref, ga_ref, gb_ref):
    u = _normed(x_ref, g_ref)
    ga_ref[...] = _sigmoid(jnp.dot(u, w_ref[:, 0:D_MODEL], preferred_element_type=F32)).astype(BF16)
    gb_ref[...] = _sigmoid(
        jnp.dot(u, w_ref[:, D_MODEL:2 * D_MODEL], preferred_element_type=F32)).astype(BF16)


def _w_cols(width, start):
    assert start % width == 0
    return pl.BlockSpec((D_MODEL, width), lambda *_: (0, start // width), pipeline_mode=pl.Buffered(1))


def _in_proj_call(body, name, x2d, gain, w_bf, w_specs, extra_in, extra_specs, out_widths, tm,
                  extra_out=(), extra_out_specs=(), scratch=()):
    rows = x2d.shape[0]
    row_spec = lambda w: pl.BlockSpec((tm, w), lambda i: (i, 0))
    return pl.pallas_call(
        body,
        grid=(rows // tm,),
        in_specs=[row_spec(D_MODEL), _const_spec(gain.shape), *w_specs, *extra_specs],
        out_specs=[*[row_spec(w) for w in out_widths], *extra_out_specs],
        out_shape=[*[jax.ShapeDtypeStruct((rows, w), BF16) for w in out_widths], *extra_out],
        scratch_shapes=list(scratch),
        compiler_params=pltpu.CompilerParams(
            dimension_semantics=("arbitrary",),
            vmem_limit_bytes=VMEM_LIMIT_BYTES),
        name=name,
    )(x2d, gain, *([w_bf] * len(w_specs)), *extra_in)


def _in_proj_all(x2d, gain, w_bf, cw, cb, tail0, cos, sin, tm, tiles_per_seq, n_pad):
    c_qk = 2 * M_INNER
    c_v = c_qk + 2 * R_QK
    c_g = c_v + R_V
    c_mix = c_g + R_V
    tail_shape = (SUBLANES, M_INNER)
    xm, xc, zs, tail = _in_proj_call(
        functools.partial(_inproj_mlstm_kernel, tm=tm, tiles_per_seq=tiles_per_seq),
        "in_proj_mlstm", x2d, gain, w_bf, [_w_cols(2 * M_INNER, 0)], (cw, cb, tail0),
        (_const_spec(cw.shape), _const_spec(cb.shape), _const_spec(tail_shape)),
        (M_INNER, M_INNER, M_INNER), tm,
        extra_out=(jax.ShapeDtypeStruct(tail_shape, F32),),
        extra_out_specs=(pl.BlockSpec(tail_shape, lambda i: (0, 0)),),
        scratch=(pltpu.VMEM(tail_shape, F32), pltpu.VMEM((2, tm, CONV_STRIP), F32)))
    half = R_QK_DIM // 2
    tmw = min(tm * PROJ_TM_WIDE // PROJ_TM, x2d.shape[0])
    tps_w = tiles_per_seq * tm // tmw
    pos_spec = pl.BlockSpec((tmw, half), lambda i: (i % tps_w, 0))
    qr, kr = _in_proj_call(
        functools.partial(_inproj_rot_kernel, tm=tmw, n_pad=n_pad),
        "in_proj_rot", x2d, gain, w_bf, [_w_cols(2 * R_QK, c_qk)], (cos, sin), (pos_spec, pos_spec),
        (R_QK, R_QK), tmw)
    v, gs = _in_proj_call(_inproj_vg_kernel, "in_proj_vg", x2d, gain, w_bf,
                          [_w_cols(R_V, c_v), _w_cols(R_V, c_g)], (), (), (R_V, R_V), tmw)
    ga, gb = _in_proj_call(_inproj_gate_kernel, "in_proj_gate", x2d, gain, w_bf,
                           [_w_cols(2 * D_MODEL, c_mix)], (), (), (D_MODEL, D_MODEL), tmw)
    return (xm, xc, zs, tail), (qr, kr, v, gs), (ga, gb)


def _head_norm(h, w):
    mu = jnp.mean(h, axis=-1, keepdims=True)
    c = h - mu
    var = jnp.mean(c * c, axis=-1, keepdims=True)
    return c * lax.rsqrt(var + EPS) * w


def _lane_cumsum(x, n):
    lane = lax.broadcasted_iota(jnp.int32, x.shape, 1)
    d = 1
    while d < n:
        x = x + jnp.where(lane >= d, pltpu.roll(x, d, axis=1), 0.0)
        d *= 2
    return x


def _mlstm_prep_kernel(xm_ref, xc_ref, bdq_ref, bdk_ref, bdv_ref, wif_ref, bif_ref,
                       q_ref, k_ref, v_ref, row_ref, col_ref, *, TP, T, n_pad):
    scale = M_HEAD_DIM ** -0.5
    for j in range(M_INNER // BD_TILE):
        sl = slice(j * BD_TILE, (j + 1) * BD_TILE)
        q_ref[:, sl] = jnp.dot(xc_ref[:, sl], bdq_ref[j], preferred_element_type=F32).astype(BF16)
        kj = jnp.dot(xc_ref[:, sl], bdk_ref[j], preferred_element_type=F32)
        k_ref[:, sl] = (kj * scale).astype(BF16)
        v_ref[:, sl] = jnp.dot(xm_ref[:, sl], bdv_ref[j], preferred_element_type=F32).astype(BF16)
    xcm = jnp.concatenate([xc_ref[...], xm_ref[...]], axis=1)
    g_all = jnp.dot(xcm, wif_ref[...], preferred_element_type=F32)[:, 0:LANES] + bif_ref[...]
    lane = lax.broadcasted_iota(jnp.int32, (T, LANES), 1)
    subl = lax.broadcasted_iota(jnp.int32, (SUBLANES, T), 0)
    for c in range(TP // T):
        g = g_all[c * T:(c + 1) * T, :]
        if n_pad:
            rowv = lax.broadcasted_iota(jnp.int32, (T, LANES), 0) >= n_pad
            g = jnp.where(jnp.logical_and(lane < M_HEADS, jnp.logical_not(rowv)), NEG, g)
        gt = g.T
        lf_t = jnp.minimum(gt, 0.0) - jnp.log1p(jnp.exp(-jnp.abs(gt)))
        cum_t = _lane_cumsum(lf_t, T)
        row_ref[c] = jnp.where(subl < M_HEADS, gt[0:SUBLANES, :], cum_t[0:SUBLANES, :])
        col_ref[c * T:(c + 1) * T, :] = jnp.where(lane < M_HEADS, g, cum_t.T)


def _mlstm_prep(xm, xc, weights, TP, T, n_pad):
    nb, L, _ = xm.shape
    blk = pl.BlockSpec((None, TP, M_INNER), lambda b, i: (b, i, 0))
    big = jax.ShapeDtypeStruct((nb, L, M_INNER), BF16)
    return pl.pallas_call(
        functools.partial(_mlstm_prep_kernel, TP=TP, T=T, n_pad=n_pad),
        grid=(nb, L // TP),
        in_specs=[blk, blk, *[_const_spec(w.shape) for w in weights]],
        out_specs=[blk, blk, blk,
                   pl.BlockSpec((None, TP // T, SUBLANES, T), lambda b, i: (b, i, 0, 0)),
                   pl.BlockSpec((None, TP, LANES), lambda b, i: (b, i, 0))],
        out_shape=[big, big, big,
                   jax.ShapeDtypeStruct((nb, L // T, SUBLANES, T), F32),
                   jax.ShapeDtypeStruct((nb, L, LANES), F32)],
        compiler_params=pltpu.CompilerParams(
            dimension_semantics=("arbitrary", "arbitrary"),
            vmem_limit_bytes=VMEM_LIMIT_BYTES),
        name="mlstm_prep",
    )(xm, xc, *weights)


def _mlstm_kernel(q_ref, k_ref, v_ref, xc_ref, zs_ref, row_ref, col_ref, skip_ref, gn_ref,
                  c0_ref, n0_ref, m0_ref, h_ref, *rest, T, emit_state):
    if emit_state:
        co_ref, no_ref, mo_ref, c_s, cb_s, n_s, m_s = rest
    else:
        c_s, cb_s, n_s, m_s = rest
    ci = pl.program_id(1)
    nc = pl.num_programs(1)

    @pl.when(ci == 0)
    def _():
        c_s[...] = c0_ref[...]
        cb_s[...] = c0_ref[...].astype(BF16)
        n_s[...] = n0_ref[...]
        m_s[...] = m0_ref[...]

    rowp = row_ref[...]
    colp = col_ref[...]
    causal = (lax.broadcasted_iota(jnp.int32, (T, T), 0)
              >= lax.broadcasted_iota(jnp.int32, (T, T), 1))

    heads = range(M_HEADS)
    sls = [slice(h * M_HEAD_DIM, (h + 1) * M_HEAD_DIM) for h in heads]

    s_raw = [lax.dot_general(q_ref[:, sls[h]], k_ref[:, sls[h]], _NT, preferred_element_type=F32)
             for h in heads]
    qc = [jnp.dot(q_ref[:, sls[h]], cb_s[h], preferred_element_type=F32) for h in heads]

    dmat, w_inter, e_negm, w_k, decay, m_new = [], [], [], [], [], []
    for h in heads:
        i_row = rowp[h:h + 1, :]
        f_row = rowp[M_HEADS + h:M_HEADS + h + 1, :]
        i_col = colp[:, h:h + 1]
        f_col = colp[:, M_HEADS + h:M_HEADS + h + 1]
        g_tot = f_row[:, T - 1:T]
        m_prev = m_s[h][0:1, 0:1]
        b_row = i_row - f_row
        log_d = jnp.where(causal, f_col + b_row, NEG)
        inter = f_col + m_prev
        m_row = jnp.maximum(inter, jnp.max(log_d, axis=1, keepdims=True))
        dmat.append(jnp.exp(log_d - m_row))
        w_inter.append(jnp.exp(inter - m_row))
        e_negm.append(jnp.exp(-m_row))
        mn = jnp.maximum(g_tot + m_prev, jnp.max(g_tot + b_row, axis=1, keepdims=True))
        m_new.append(mn)
        w_k.append(jnp.exp(g_tot - f_col + i_col - mn))
        decay.append(jnp.exp(g_tot + m_prev - mn))

    hhs = []
    for h in heads:
        qn = jnp.sum(q_ref[:, sls[h]].astype(F32) * n_s[h][0:1, :], axis=1, keepdims=True)
        s = s_raw[h] * dmat[h]
        num = (jnp.dot(s.astype(BF16), v_ref[:, sls[h]], preferred_element_type=F32)
               + w_inter[h] * qc[h])
        den = jnp.sum(s, axis=1, keepdims=True) + w_inter[h] * qn
        hhs.append(num * (1.0 / jnp.maximum(jnp.abs(den), e_negm[h])))
    for h in heads:
        kw = k_ref[:, sls[h]].astype(F32) * w_k[h]
        c_new = decay[h] * c_s[h] + lax.dot_general(kw.astype(BF16), v_ref[:, sls[h]], _TN,
                                                    preferred_element_type=F32)
        n_new = decay[h] * n_s[h][0:1, :] + jnp.sum(kw, axis=0, keepdims=True)
        c_s[h] = c_new
        cb_s[h] = c_new.astype(BF16)
        n_s[h] = jnp.broadcast_to(n_new, (SUBLANES, M_HEAD_DIM))
        m_s[h] = jnp.broadcast_to(m_new[h], (SUBLANES, LANES))
    for h in heads:
        hn = _head_norm(hhs[h], gn_ref[:, sls[h]])
        out = ((hn + skip_ref[:, sls[h]] * xc_ref[:, sls[h]].astype(F32))
               * zs_ref[:, sls[h]].astype(F32))
        h_ref[:, sls[h]] = out.astype(BF16)

    if emit_state:
        @pl.when(ci == nc - 1)
        def _():
            co_ref[...] = c_s[...]
            no_ref[...] = n_s[...]
            mo_ref[...] = m_s[...]


def _mlstm(q, k, v, xc, zs, rowp, colp, skip, gn, state, T, emit_state):
    nb, L, _ = q.shape
    nc = L // T
    blk = pl.BlockSpec((None, T, M_INNER), lambda b, c: (b, c, 0))
    in_specs = [blk, blk, blk, blk, blk,
                pl.BlockSpec((None, None, SUBLANES, T), lambda b, c: (b, c, 0, 0)),
                pl.BlockSpec((None, T, LANES), lambda b, c: (b, c, 0)),
                _const_spec(skip.shape), _const_spec(gn.shape),
                *[_const_spec(s.shape) for s in state]]
    out_shape = [jax.ShapeDtypeStruct((nb, L, M_INNER), BF16)]
    out_specs = [blk]
    if emit_state:
        assert nb == 1
        for s in state:
            out_shape.append(jax.ShapeDtypeStruct(s.shape, F32))
            out_specs.append(pl.BlockSpec(s.shape, lambda b, c, _n=len(s.shape): (0,) * _n))
    c_shape = state[0].shape
    scratch = [pltpu.VMEM(c_shape, F32), pltpu.VMEM(c_shape, BF16),
               pltpu.VMEM(state[1].shape, F32), pltpu.VMEM(state[2].shape, F32)]
    return pl.pallas_call(
        functools.partial(_mlstm_kernel, T=T, emit_state=emit_state),
        grid=(nb, nc),
        in_specs=in_specs,
        out_specs=out_specs,
        out_shape=out_shape,
        scratch_shapes=scratch,
        compiler_params=pltpu.CompilerParams(
            dimension_semantics=("arbitrary", "arbitrary"),
            vmem_limit_bytes=VMEM_LIMIT_BYTES),
        name="mlstm_state" if emit_state else "mlstm",
    )(q, k, v, xc, zs, rowp, colp, skip, gn, *state)


def _retention_kernel(q_ref, k_ref, v_ref, gs_ref, gn_ref, s0_ref,
                      o_ref, *rest, T, emit_state):
    if emit_state:
        so_ref, st_s, dm_s = rest
    else:
        st_s, dm_s = rest
    ci = pl.program_id(1)
    nc = pl.num_programs(1)
    log_gamma = [math.log(1.0 - 2.0 ** (-5.0 - h)) for h in range(R_HEADS)]

    @pl.when(ci == 0)
    def _():
        st_s[...] = s0_ref[...]
        diff = (lax.broadcasted_iota(jnp.int32, (T, T), 0)
                - lax.broadcasted_iota(jnp.int32, (T, T), 1))
        dfl = diff.astype(F32)
        for h in range(R_HEADS):
            dm_s[h] = jnp.where(diff >= 0, jnp.exp(log_gamma[h] * dfl), 0.0)

    tf = lax.broadcasted_iota(jnp.int32, (T, 1), 0).astype(F32)
    for h in range(R_HEADS):
        qh = q_ref[:, h * R_QK_DIM:(h + 1) * R_QK_DIM]
        kh = k_ref[:, h * R_QK_DIM:(h + 1) * R_QK_DIM]
        sl = slice(h * R_V_DIM, (h + 1) * R_V_DIM)
        vh = v_ref[:, sl]
        s = lax.dot_general(qh, kh, _NT, preferred_element_type=F32) * dm_s[h]
        q_decay = jnp.exp(log_gamma[h] * (tf + 1.0))
        k_decay = jnp.exp(log_gamma[h] * (T - 1.0 - tf))
        o = (jnp.dot(s.astype(BF16), vh, preferred_element_type=F32)
             + q_decay * jnp.dot(qh, st_s[h].astype(BF16), preferred_element_type=F32))
        kd = (kh.astype(F32) * k_decay).astype(BF16)
        st_s[h] = (math.exp(log_gamma[h] * T) * st_s[h]
                   + lax.dot_general(kd, vh, _TN, preferred_element_type=F32))
        o_ref[:, sl] = (_head_norm(o, gn_ref[:, sl]) * gs_ref[:, sl].astype(F32)).astype(BF16)

    if emit_state:
        @pl.when(ci == nc - 1)
        def _():
            so_ref[...] = st_s[...]


def _retention(q, k, v, gs, gn, s0, T, emit_state):
    nb, L, _ = q.shape
    nc = L // T
    qk_blk = pl.BlockSpec((None, T, R_QK), lambda b, c: (b, c, 0))
    v_blk = pl.BlockSpec((None, T, R_V), lambda b, c: (b, c, 0))
    in_specs = [qk_blk, qk_blk, v_blk, v_blk, _const_spec(gn.shape), _const_spec(s0.shape)]
    out_shape = [jax.ShapeDtypeStruct((nb, L, R_V), BF16)]
    out_specs = [v_blk]
    if emit_state:
        assert nb == 1
        out_shape.append(jax.ShapeDtypeStruct(s0.shape, F32))
        out_specs.append(pl.BlockSpec(s0.shape, lambda b, c: (0, 0, 0)))
    scratch = [pltpu.VMEM(s0.shape, F32), pltpu.VMEM((R_HEADS, T, T), F32)]
    return pl.pallas_call(
        functools.partial(_retention_kernel, T=T, emit_state=emit_state),
        grid=(nb, nc),
        in_specs=in_specs,
        out_specs=out_specs,
        out_shape=out_shape,
        scratch_shapes=scratch,
        compiler_params=pltpu.CompilerParams(
            dimension_semantics=("arbitrary", "arbitrary"),
            vmem_limit_bytes=VMEM_LIMIT_BYTES),
        name="retention_state" if emit_state else "retention",
    )(q, k, v, gs, gn, s0)


def _rms(x, g):
    return x * lax.rsqrt(jnp.mean(x * x, axis=-1, keepdims=True) + EPS) * g


def _out_ffn_kernel(hm_ref, hr_ref, ga_ref, gb_ref, x_ref, wpm_ref, wpr_ref, wout_ref,
                    wup_ref, wdn_ref, g1_ref, g2_ref, g3_ref, o_ref):
    ya = jnp.dot(hm_ref[...], wpm_ref[...], preferred_element_type=F32)
    yb = jnp.dot(hr_ref[...], wpr_ref[...], preferred_element_type=F32)
    mixin = ga_ref[...].astype(F32) * ya + gb_ref[...].astype(F32) * yb
    mix = jnp.dot(mixin.astype(BF16), wout_ref[...], preferred_element_type=F32)
    h1 = x_ref[...] + _rms(mix, g1_ref[...])
    u = _rms(h1, g2_ref[...]).astype(BF16)
    f = None
    for j in range(D_FF // FF_TILE):
        sl = slice(j * FF_TILE, (j + 1) * FF_TILE)
        a = jnp.maximum(jnp.dot(u, wup_ref[:, sl], preferred_element_type=F32), 0.0)
        part = jnp.dot((a * a).astype(BF16), wdn_ref[sl, :], preferred_element_type=F32)
        f = part if f is None else f + part
    o_ref[...] = h1 + _rms(f, g3_ref[...])


def _out_ffn(hm, hr, ga, gb, x2d, wpm, wpr, wout, wup, wdn, g1, g2, g3, tm):
    rows = x2d.shape[0]
    row_spec = lambda w: pl.BlockSpec((tm, w), lambda i: (i, 0))
    consts = (wpm, wpr, wout, wup, wdn, g1, g2, g3)
    return pl.pallas_call(
        _out_ffn_kernel,
        grid=(rows // tm,),
        in_specs=[row_spec(M_INNER), row_spec(R_V), row_spec(D_MODEL), row_spec(D_MODEL),
                  row_spec(D_MODEL), *[_const_spec(c.shape) for c in consts]],
        out_specs=row_spec(D_MODEL),
        out_shape=jax.ShapeDtypeStruct((rows, D_MODEL), F32),
        compiler_params=pltpu.CompilerParams(
            dimension_semantics=("arbitrary",),
            vmem_limit_bytes=VMEM_LIMIT_BYTES),
        name="out_ffn",
    )(hm, hr, ga, gb, x2d, *consts)


def _block_diag_tiles(w):
    rows = w.reshape(M_INNER, M_QKV_BLOCK)
    wide = jnp.tile(rows, (1, BD_TILE // M_QKV_BLOCK))
    r_blk = (lax.broadcasted_iota(jnp.int32, wide.shape, 0) % BD_TILE) // M_QKV_BLOCK
    c_blk = lax.broadcasted_iota(jnp.int32, wide.shape, 1) // M_QKV_BLOCK
    t = jnp.where(r_blk == c_blk, wide, 0.0).astype(BF16)
    return t.reshape(M_INNER // BD_TILE, BD_TILE, BD_TILE)


def _rope_tables(pos0, n):
    half = R_QK_DIM // 2
    inv = ROPE_BASE ** (-np.arange(half, dtype=np.float64) / half)
    ang = (pos0 + np.arange(n, dtype=np.float64))[:, None] * inv[None, :]
    return jnp.asarray(np.cos(ang), F32), jnp.asarray(np.sin(ang), F32)


def kernel(x, meta_tokens, norm_mix_pre, w_in, conv_w, conv_b, w_q_m, w_k_m, w_v_m, w_if, b_if,
           skip_m, gn_m, gn_r, w_proj_m, w_proj_r, w_out, norm_mix_post, norm_ffn_pre, w_up,
           w_down, norm_ffn_post):
    nb, seq, _ = x.shape
    assert norm_mix_pre.shape[0] == 1, "single-layer block"
    assert seq % PROJ_TM_WIDE == 0 and seq % CHUNK_T == 0
    row = lambda a: a[0].reshape(1, -1).astype(F32)
    n_pad = META_T - N_META

    w_in_bf = w_in[0].astype(BF16)
    g_pre = row(norm_mix_pre)
    cw, cb = conv_w[0].astype(F32), row(conv_b)

    meta_rows = jnp.concatenate(
        [jnp.zeros((n_pad, D_MODEL), x.dtype), meta_tokens.astype(x.dtype)], axis=0)
    cos_m, sin_m = _rope_tables(-n_pad, META_T)
    cos_x, sin_x = _rope_tables(N_META, seq)
    tail_zero = jnp.zeros((SUBLANES, M_INNER), F32)
    m_meta, r_meta, _ = _in_proj_all(meta_rows, g_pre, w_in_bf, cw, cb, tail_zero, cos_m, sin_m,
                                     META_T, 1, n_pad)
    x2d = x.reshape(nb * seq, D_MODEL)
    m_in, r_in, (ga, gb) = _in_proj_all(x2d, g_pre, w_in_bf, cw, cb, m_meta[3], cos_x, sin_x,
                                        PROJ_TM, seq // PROJ_TM, 0)

    nblk = M_INNER // M_QKV_BLOCK
    wif3 = w_if[0].reshape(3, nblk, M_QKV_BLOCK, 2 * M_HEADS)
    fold = lambda w, part: jnp.einsum('nio,noj->nij', w, part).reshape(M_INNER, 2 * M_HEADS)
    wg = jnp.concatenate([fold(w_q_m[0], wif3[0]) + fold(w_k_m[0], wif3[1]), fold(w_v_m[0], wif3[2])],
                         axis=0)
    wif = jnp.zeros((2 * M_INNER, GATE_PAD), F32).at[:, :2 * M_HEADS].set(wg).astype(BF16)
    bif = jnp.zeros((1, LANES), F32).at[0, :2 * M_HEADS].set(b_if[0])
    p_weights = (_block_diag_tiles(w_q_m[0]), _block_diag_tiles(w_k_m[0]),
                 _block_diag_tiles(w_v_m[0]), wif, bif)
    m_zero = (jnp.zeros((M_HEADS, M_HEAD_DIM, M_HEAD_DIM), F32),
              jnp.zeros((M_HEADS, SUBLANES, M_HEAD_DIM), F32),
              jnp.zeros((M_HEADS, SUBLANES, LANES), F32))
    as_seq = lambda a, n, t: a.reshape(n, t, a.shape[-1])
    xm_m, xc_m, zs_m = [as_seq(a, 1, META_T) for a in m_meta[:3]]
    xm_x, xc_x, zs_x = [as_seq(a, nb, seq) for a in m_in[:3]]
    prep_m = _mlstm_prep(xm_m, xc_m, p_weights, META_T, META_T, n_pad)
    m_init = _mlstm(*prep_m[:3], xc_m, zs_m, *prep_m[3:], row(skip_m), row(gn_m), m_zero,
                    META_T, True)[1:]
    prep_x = _mlstm_prep(xm_x, xc_x, p_weights, PREP_TP, CHUNK_T, 0)
    hm = _mlstm(*prep_x[:3], xc_x, zs_x, *prep_x[3:], row(skip_m), row(gn_m), tuple(m_init),
                CHUNK_T, False)[0]

    r_zero = jnp.zeros((R_HEADS, R_QK_DIM, R_V_DIM), F32)
    r_init = _retention(*[as_seq(a, 1, META_T) for a in r_meta], row(gn_r), r_zero,
                        META_T, True)[1]
    hr = _retention(*[as_seq(a, nb, seq) for a in r_in], row(gn_r), r_init, CHUNK_T, False)[0]

    out = _out_ffn(hm.reshape(nb * seq, M_INNER), hr.reshape(nb * seq, R_V), ga, gb, x2d,
                   w_proj_m[0].astype(BF16), w_proj_r[0].astype(BF16), w_out[0].astype(BF16),
                   w_up[0].astype(BF16), w_down[0].astype(BF16),
                   row(norm_mix_post), row(norm_ffn_pre), row(norm_ffn_post), OUT_TM)
    return out.reshape(nb, seq, D_MODEL)
```

```python
import functools
import math

import jax
import jax.numpy as jnp
import numpy as np
from jax import lax
from jax.experimental import pallas as pl
from jax.experimental.pallas import tpu as pltpu

F32 = jnp.float32
BF16 = jnp.bfloat16

D_MODEL = 1024
N_META = 16
M_INNER = 2 * D_MODEL
M_HEADS = 4
M_HEAD_DIM = M_INNER // M_HEADS
M_QKV_BLOCK = 4
M_CONV = 4
R_HEADS = 4
R_QK_DIM = D_MODEL // R_HEADS
R_V_DIM = 2 * R_QK_DIM
R_QK = R_HEADS * R_QK_DIM
R_V = R_HEADS * R_V_DIM
D_FF = 4 * D_MODEL
ROPE_BASE = 10000.0
EPS = 1e-6
NEG = -1e30
LOG2E = math.log2(math.e)
N_IN = 2 * M_INNER + 2 * R_QK + 2 * R_V + 2 * D_MODEL

LANES = 128
SUBLANES = 8
MXU_DIM = 256
VMEM_LIMIT_BYTES = 56 * 1024 * 1024

CHUNK_T = 256
META_T = 128
BD_TILE = MXU_DIM
GATE_PAD = MXU_DIM
STEP_CHUNKS = 2
PREP_TP = 512
CONV_STRIP = 256
PROJ_TM = 512
PROJ_TM_WIDE = 1024
OUT_TM = 512
FF_TILE = 1024

_NT = (((1,), (1,)), ((), ()))
_TN = (((0,), (0,)), ((), ()))


def _const_spec(shape):
    nd = len(shape)
    return pl.BlockSpec(shape, lambda *_: (0,) * nd, pipeline_mode=pl.Buffered(1))


def _sigmoid(x):
    return 1.0 / (1.0 + jnp.exp2(x * (-LOG2E)))


def _silu(x):
    return x * _sigmoid(x)


def _normed(x_ref, g_ref):
    x = x_ref[...]
    ms = jnp.mean(x * x, axis=-1, keepdims=True)
    return (x * lax.rsqrt(ms + EPS) * g_ref[...]).astype(BF16)


def _inproj_mlstm_kernel(x_ref, g_ref, w_ref, cw_ref, cb_ref, t0_ref,
                         xm_ref, xc_ref, zs_ref, to_ref, halo_s, res_s, *, tm, tiles_per_seq):
    i = pl.program_id(0)

    @pl.when(i % tiles_per_seq == 0)
    def _():
        halo_s[...] = t0_ref[...]

    u = _normed(x_ref, g_ref)
    nstrip = M_INNER // CONV_STRIP
    nv = tm // SUBLANES
    sub = lax.broadcasted_iota(jnp.int32, (nv, SUBLANES, CONV_STRIP), 1)

    def strip_dot(c):
        sl = slice(c * CONV_STRIP, (c + 1) * CONV_STRIP)
        res_s[(i + c) % 2] = jnp.dot(u, w_ref[:, sl], preferred_element_type=F32)

    strip_dot(0)
    for c in range(nstrip):
        sl = slice(c * CONV_STRIP, (c + 1) * CONV_STRIP)
        if c + 1 < nstrip:
            strip_dot(c + 1)
        zsl = slice(M_INNER + c * CONV_STRIP, M_INNER + (c + 1) * CONV_STRIP)
        zm = jnp.dot(u, w_ref[:, zsl], preferred_element_type=F32)
        zs_ref[:, sl] = _silu(zm).astype(BF16)
        xm = res_s[(i + c) % 2]
        xm_ref[:, sl] = xm.astype(BF16)
        x3 = xm.reshape(nv, SUBLANES, CONV_STRIP)
        prev3 = jnp.concatenate([halo_s[:, sl].reshape(1, SUBLANES, CONV_STRIP), x3[:nv - 1]], axis=0)
        y = cb_ref[:, sl] + cw_ref[M_CONV - 1:M_CONV, sl] * x3
        for j in range(M_CONV - 1):
            sh = M_CONV - 1 - j
            merged = jnp.where(sub >= SUBLANES - sh, prev3, x3)
            y = y + cw_ref[j:j + 1, sl] * pltpu.roll(merged, sh, axis=1)
        halo_s[:, sl] = xm[tm - SUBLANES:, :]
        xc_ref[:, sl] = _silu(y.reshape(tm, CONV_STRIP)).astype(BF16)

    @pl.when(i == pl.num_programs(0) - 1)
    def _():
        to_ref[...] = halo_s[...]


def _inproj_rot_gate_kernel(x_ref, g_ref, w_ref, wmix_ref, cos_ref, sin_ref,
                            q_ref, k_ref, ga_ref, gb_ref, *, tm, n_pad):
    u = _normed(x_ref, g_ref)
    cos = cos_ref[...]
    sin = sin_ref[...]
    half = R_QK_DIM // 2
    kscale = R_QK_DIM ** -0.5
    if n_pad:
        kmul = (lax.broadcasted_iota(jnp.int32, (tm, 1), 0) >= n_pad).astype(F32) * kscale

    def rot(t, h):
        t1 = t[:, h * R_QK_DIM:h * R_QK_DIM + half]
        t2 = t[:, h * R_QK_DIM + half:(h + 1) * R_QK_DIM]
        return t1 * cos - t2 * sin, t1 * sin + t2 * cos

    q = jnp.dot(u, w_ref[:, 0:R_QK], preferred_element_type=F32)
    for h in range(R_HEADS):
        a, b = rot(q, h)
        q_ref[:, h * R_QK_DIM:h * R_QK_DIM + half] = a.astype(BF16)
        q_ref[:, h * R_QK_DIM + half:(h + 1) * R_QK_DIM] = b.astype(BF16)
    k = jnp.dot(u, w_ref[:, R_QK:2 * R_QK], preferred_element_type=F32)
    for h in range(R_HEADS):
        a, b = rot(k, h)
        if n_pad:
            a, b = a * kmul, b * kmul
        else:
            a, b = a * kscale, b * kscale
        k_ref[:, h * R_QK_DIM:h * R_QK_DIM + half] = a.astype(BF16)
        k_ref[:, h * R_QK_DIM + half:(h + 1) * R_QK_DIM] = b.astype(BF16)
    ga_ref[...] = _sigmoid(jnp.dot(u, wmix_ref[:, 0:D_MODEL], preferred_element_type=F32)).astype(BF16)
    gb_ref[...] = _sigmoid(
        jnp.dot(u, wmix_ref[:, D_MODEL:2 * D_MODEL], preferred_element_type=F32)).astype(BF16)


def _inproj_vg_kernel(x_ref, g_ref, wv_ref, wg_ref, v_ref, gs_ref):
    u = _normed(x_ref, g_ref)
    v_ref[...] = jnp.dot(u, wv_ref[...], preferred_element_type=F32).astype(BF16)
    gs_ref[...] = _silu(jnp.dot(u, wg_ref[...], preferred_element_type=F32)).astype(BF16)


def _w_cols(width, start):
    assert start % width == 0
    return pl.BlockSpec((D_MODEL, width), lambda *_: (0, start // width), pipeline_mode=pl.Buffered(1))


def _in_proj_call(body, name, x2d, gain, w_bf, w_specs, extra_in, extra_specs, out_widths, tm,
                  extra_out=(), extra_out_specs=(), scratch=()):
    rows = x2d.shape[0]
    row_spec = lambda w: pl.BlockSpec((tm, w), lambda i: (i, 0))
    return pl.pallas_call(
        body,
        grid=(rows // tm,),
        in_specs=[row_spec(D_MODEL), _const_spec(gain.shape), *w_specs, *extra_specs],
        out_specs=[*[row_spec(w) for w in out_widths], *extra_out_specs],
        out_shape=[*[jax.ShapeDtypeStruct((rows, w), BF16) for w in out_widths], *extra_out],
        scratch_shapes=list(scratch),
        compiler_params=pltpu.CompilerParams(
            dimension_semantics=("arbitrary",),
            vmem_limit_bytes=VMEM_LIMIT_BYTES),
        name=name,
    )(x2d, gain, *([w_bf] * len(w_specs)), *extra_in)


def _in_proj_all(x2d, gain, w_bf, cw, cb, tail0, cos, sin, tm, tiles_per_seq, n_pad):
    c_qk = 2 * M_INNER
    c_v = c_qk + 2 * R_QK
    c_g = c_v + R_V
    c_mix = c_g + R_V
    tail_shape = (SUBLANES, M_INNER)
    xm, xc, zs, tail = _in_proj_call(
        functools.partial(_inproj_mlstm_kernel, tm=tm, tiles_per_seq=tiles_per_seq),
        "in_proj_mlstm", x2d, gain, w_bf, [_w_cols(2 * M_INNER, 0)], (cw, cb, tail0),
        (_const_spec(cw.shape), _const_spec(cb.shape), _const_spec(tail_shape)),
        (M_INNER, M_INNER, M_INNER), tm,
        extra_out=(jax.ShapeDtypeStruct(tail_shape, F32),),
        extra_out_specs=(pl.BlockSpec(tail_shape, lambda i: (0, 0)),),
        scratch=(pltpu.VMEM(tail_shape, F32), pltpu.VMEM((2, tm, CONV_STRIP), F32)))
    half = R_QK_DIM // 2
    tmw = min(tm * PROJ_TM_WIDE // PROJ_TM, x2d.shape[0])
    tps_w = tiles_per_seq * tm // tmw
    pos_spec = pl.BlockSpec((tmw, half), lambda i: (i % tps_w, 0))
    qr, kr, ga, gb = _in_proj_call(
        functools.partial(_inproj_rot_gate_kernel, tm=tmw, n_pad=n_pad),
        "in_proj_rot_gate", x2d, gain, w_bf, [_w_cols(2 * R_QK, c_qk), _w_cols(2 * D_MODEL, c_mix)],
        (cos, sin), (pos_spec, pos_spec), (R_QK, R_QK, D_MODEL, D_MODEL), tmw)
    v, gs = _in_proj_call(_inproj_vg_kernel, "in_proj_vg", x2d, gain, w_bf,
                          [_w_cols(R_V, c_v), _w_cols(R_V, c_g)], (), (), (R_V, R_V), tmw)
    return (xm, xc, zs, tail), (qr, kr, v, gs), (ga, gb)


def _head_norm(h, w):
    mu = jnp.mean(h, axis=-1, keepdims=True)
    c = h - mu
    var = jnp.mean(c * c, axis=-1, keepdims=True)
    return c * lax.rsqrt(var + EPS) * w


def _lane_cumsum(x, n):
    lane = lax.broadcasted_iota(jnp.int32, x.shape, 1)
    d = 1
    while d < n:
        x = x + jnp.where(lane >= d, pltpu.roll(x, d, axis=1), 0.0)
        d *= 2
    return x


def _mlstm_prep_kernel(xm_ref, xc_ref, bdq_ref, bdk_ref, bdv_ref, wif_ref, bif_ref,
                       q_ref, k_ref, v_ref, row_ref, col_ref, *, TP, T, n_pad):
    scale = M_HEAD_DIM ** -0.5
    for j in range(M_INNER // BD_TILE):
        sl = slice(j * BD_TILE, (j + 1) * BD_TILE)
        q_ref[:, sl] = jnp.dot(xc_ref[:, sl], bdq_ref[j], preferred_element_type=F32).astype(BF16)
        kj = jnp.dot(xc_ref[:, sl], bdk_ref[j], preferred_element_type=F32)
        k_ref[:, sl] = (kj * scale).astype(BF16)
        v_ref[:, sl] = jnp.dot(xm_ref[:, sl], bdv_ref[j], preferred_element_type=F32).astype(BF16)
    xcm = jnp.concatenate([xc_ref[...], xm_ref[...]], axis=1)
    g_all = jnp.dot(xcm, wif_ref[...], preferred_element_type=F32)[:, 0:LANES] + bif_ref[...]
    lane = lax.broadcasted_iota(jnp.int32, (T, LANES), 1)
    subl = lax.broadcasted_iota(jnp.int32, (SUBLANES, T), 0)
    for c in range(TP // T):
        g = g_all[c * T:(c + 1) * T, :]
        if n_pad:
            rowv = lax.broadcasted_iota(jnp.int32, (T, LANES), 0) >= n_pad
            g = jnp.where(jnp.logical_and(lane < M_HEADS, jnp.logical_not(rowv)), NEG, g)
        gt = g.T
        lf_t = jnp.minimum(gt, 0.0) - jnp.log1p(jnp.exp(-jnp.abs(gt)))
        cum_t = _lane_cumsum(lf_t, T)
        row_ref[c] = jnp.where(subl < M_HEADS, gt[0:SUBLANES, :], cum_t[0:SUBLANES, :])
        col_ref[c * T:(c + 1) * T, :] = jnp.where(lane < M_HEADS, g, cum_t.T)


def _mlstm_prep(xm, xc, weights, TP, T, n_pad):
    nb, L, _ = xm.shape
    blk = pl.BlockSpec((None, TP, M_INNER), lambda b, i: (b, i, 0))
    big = jax.ShapeDtypeStruct((nb, L, M_INNER), BF16)
    return pl.pallas_call(
        functools.partial(_mlstm_prep_kernel, TP=TP, T=T, n_pad=n_pad),
        grid=(nb, L // TP),
        in_specs=[blk, blk, *[_const_spec(w.shape) for w in weights]],
        out_specs=[blk, blk, blk,
                   pl.BlockSpec((None, TP // T, SUBLANES, T), lambda b, i: (b, i, 0, 0)),
                   pl.BlockSpec((None, TP, LANES), lambda b, i: (b, i, 0))],
        out_shape=[big, big, big,
                   jax.ShapeDtypeStruct((nb, L // T, SUBLANES, T), F32),
                   jax.ShapeDtypeStruct((nb, L, LANES), F32)],
        compiler_params=pltpu.CompilerParams(
            dimension_semantics=("arbitrary", "arbitrary"),
            vmem_limit_bytes=VMEM_LIMIT_BYTES),
        name="mlstm_prep",
    )(xm, xc, *weights)


def _mlstm_kernel(q_ref, k_ref, v_ref, xc_ref, zs_ref, row_ref, col_ref, skip_ref, gn_ref,
                  c0_ref, n0_ref, m0_ref, h_ref, *rest, T, nsub, emit_state):
    if emit_state:
        co_ref, no_ref, mo_ref, c_s, cb_s, n_s, m_s = rest
    else:
        c_s, cb_s, n_s, m_s = rest
    ci = pl.program_id(1)
    nc = pl.num_programs(1)

    @pl.when(ci == 0)
    def _():
        c_s[...] = c0_ref[...]
        cb_s[...] = c0_ref[...].astype(BF16)
        n_s[...] = n0_ref[...]
        m_s[...] = m0_ref[...]

    causal = (lax.broadcasted_iota(jnp.int32, (T, T), 0)
              >= lax.broadcasted_iota(jnp.int32, (T, T), 1))
    heads = range(M_HEADS)
    sls = [slice(h * M_HEAD_DIM, (h + 1) * M_HEAD_DIM) for h in heads]

    def chunk(rows, rowp):
        colp = col_ref[rows, :]

        s_raw = [lax.dot_general(q_ref[rows, sls[h]], k_ref[rows, sls[h]], _NT,
                                 preferred_element_type=F32) for h in heads]
        qc = [jnp.dot(q_ref[rows, sls[h]], cb_s[h], preferred_element_type=F32) for h in heads]

        dmat, w_inter, e_negm, w_k, decay, m_new = [], [], [], [], [], []
        for h in heads:
            i_row = rowp[h:h + 1, :]
            f_row = rowp[M_HEADS + h:M_HEADS + h + 1, :]
            i_col = colp[:, h:h + 1]
            f_col = colp[:, M_HEADS + h:M_HEADS + h + 1]
            g_tot = f_row[:, T - 1:T]
            m_prev = m_s[h][0:1, 0:1]
            b_row = i_row - f_row
            log_d = jnp.where(causal, f_col + b_row, NEG)
            inter = f_col + m_prev
            m_row = jnp.maximum(inter, jnp.max(log_d, axis=1, keepdims=True))
            dmat.append(jnp.exp(log_d - m_row))
            w_inter.append(jnp.exp(inter - m_row))
            e_negm.append(jnp.exp(-m_row))
            mn = jnp.maximum(g_tot + m_prev, jnp.max(g_tot + b_row, axis=1, keepdims=True))
            m_new.append(mn)
            w_k.append(jnp.exp(g_tot - f_col + i_col - mn))
            decay.append(jnp.exp(g_tot + m_prev - mn))

        hhs = []
        for h in heads:
            qn = jnp.sum(q_ref[rows, sls[h]].astype(F32) * n_s[h][0:1, :], axis=1, keepdims=True)
            s = s_raw[h] * dmat[h]
            num = (jnp.dot(s.astype(BF16), v_ref[rows, sls[h]], preferred_element_type=F32)
                   + w_inter[h] * qc[h])
            den = jnp.sum(s, axis=1, keepdims=True) + w_inter[h] * qn
            hhs.append(num * (1.0 / jnp.maximum(jnp.abs(den), e_negm[h])))
        for h in heads:
            kw = k_ref[rows, sls[h]].astype(F32) * w_k[h]
            c_new = decay[h] * c_s[h] + lax.dot_general(kw.astype(BF16), v_ref[rows, sls[h]], _TN,
                                                        preferred_element_type=F32)
            n_new = decay[h] * n_s[h][0:1, :] + jnp.sum(kw, axis=0, keepdims=True)
            c_s[h] = c_new
            cb_s[h] = c_new.astype(BF16)
            n_s[h] = jnp.broadcast_to(n_new, (SUBLANES, M_HEAD_DIM))
            m_s[h] = jnp.broadcast_to(m_new[h], (SUBLANES, LANES))
        for h in heads:
            hn = _head_norm(hhs[h], gn_ref[:, sls[h]])
            out = ((hn + skip_ref[:, sls[h]] * xc_ref[rows, sls[h]].astype(F32))
                   * zs_ref[rows, sls[h]].astype(F32))
            h_ref[rows, sls[h]] = out.astype(BF16)

    for cc in range(nsub):
        chunk(slice(cc * T, (cc + 1) * T), row_ref[cc])

    if emit_state:
        @pl.when(ci == nc - 1)
        def _():
            co_ref[...] = c_s[...]
            no_ref[...] = n_s[...]
            mo_ref[...] = m_s[...]


def _mlstm(q, k, v, xc, zs, rowp, colp, skip, gn, state, T, emit_state):
    nb, L, _ = q.shape
    nsub = min(STEP_CHUNKS, L // T)
    nc = L // (T * nsub)
    blk = pl.BlockSpec((None, T * nsub, M_INNER), lambda b, c: (b, c, 0))
    in_specs = [blk, blk, blk, blk, blk,
                pl.BlockSpec((None, nsub, SUBLANES, T), lambda b, c: (b, c, 0, 0)),
                pl.BlockSpec((None, T * nsub, LANES), lambda b, c: (b, c, 0)),
                _const_spec(skip.shape), _const_spec(gn.shape),
                *[_const_spec(s.shape) for s in state]]
    out_shape = [jax.ShapeDtypeStruct((nb, L, M_INNER), BF16)]
    out_specs = [blk]
    if emit_state:
        assert nb == 1
        for s in state:
            out_shape.append(jax.ShapeDtypeStruct(s.shape, F32))
            out_specs.append(pl.BlockSpec(s.shape, lambda b, c, _n=len(s.shape): (0,) * _n))
    c_shape = state[0].shape
    scratch = [pltpu.VMEM(c_shape, F32), pltpu.VMEM(c_shape, BF16),
               pltpu.VMEM(state[1].shape, F32), pltpu.VMEM(state[2].shape, F32)]
    return pl.pallas_call(
        functools.partial(_mlstm_kernel, T=T, nsub=nsub, emit_state=emit_state),
        grid=(nb, nc),
        in_specs=in_specs,
        out_specs=out_specs,
        out_shape=out_shape,
        scratch_shapes=scratch,
        compiler_params=pltpu.CompilerParams(
            dimension_semantics=("arbitrary", "arbitrary"),
            vmem_limit_bytes=VMEM_LIMIT_BYTES),
        name="mlstm_state" if emit_state else "mlstm",
    )(q, k, v, xc, zs, rowp, colp, skip, gn, *state)


def _retention_kernel(q_ref, k_ref, v_ref, gs_ref, gn_ref, s0_ref,
                      o_ref, *rest, T, nsub, emit_state):
    if emit_state:
        so_ref, st_s, dm_s = rest
    else:
        st_s, dm_s = rest
    ci = pl.program_id(1)
    nc = pl.num_programs(1)
    log_gamma = [math.log(1.0 - 2.0 ** (-5.0 - h)) for h in range(R_HEADS)]

    @pl.when(ci == 0)
    def _():
        st_s[...] = s0_ref[...]
        diff = (lax.broadcasted_iota(jnp.int32, (T, T), 0)
                - lax.broadcasted_iota(jnp.int32, (T, T), 1))
        dfl = diff.astype(F32)
        for h in range(R_HEADS):
            dm_s[h] = jnp.where(diff >= 0, jnp.exp(log_gamma[h] * dfl), 0.0)

    tf = lax.broadcasted_iota(jnp.int32, (T, 1), 0).astype(F32)
    for cc in range(nsub):
        rows = slice(cc * T, (cc + 1) * T)
        for h in range(R_HEADS):
            qh = q_ref[rows, h * R_QK_DIM:(h + 1) * R_QK_DIM]
            kh = k_ref[rows, h * R_QK_DIM:(h + 1) * R_QK_DIM]
            sl = slice(h * R_V_DIM, (h + 1) * R_V_DIM)
            vh = v_ref[rows, sl]
            s = lax.dot_general(qh, kh, _NT, preferred_element_type=F32) * dm_s[h]
            q_decay = jnp.exp(log_gamma[h] * (tf + 1.0))
            k_decay = jnp.exp(log_gamma[h] * (T - 1.0 - tf))
            o = (jnp.dot(s.astype(BF16), vh, preferred_element_type=F32)
                 + q_decay * jnp.dot(qh, st_s[h].astype(BF16), preferred_element_type=F32))
            kd = (kh.astype(F32) * k_decay).astype(BF16)
            st_s[h] = (math.exp(log_gamma[h] * T) * st_s[h]
                       + lax.dot_general(kd, vh, _TN, preferred_element_type=F32))
            o_ref[rows, sl] = (_head_norm(o, gn_ref[:, sl])
                               * gs_ref[rows, sl].astype(F32)).astype(BF16)

    if emit_state:
        @pl.when(ci == nc - 1)
        def _():
            so_ref[...] = st_s[...]


def _retention(q, k, v, gs, gn, s0, T, emit_state):
    nb, L, _ = q.shape
    nsub = min(STEP_CHUNKS, L // T)
    nc = L // (T * nsub)
    qk_blk = pl.BlockSpec((None, T * nsub, R_QK), lambda b, c: (b, c, 0))
    v_blk = pl.BlockSpec((None, T * nsub, R_V), lambda b, c: (b, c, 0))
    in_specs = [qk_blk, qk_blk, v_blk, v_blk, _const_spec(gn.shape), _const_spec(s0.shape)]
    out_shape = [jax.ShapeDtypeStruct((nb, L, R_V), BF16)]
    out_specs = [v_blk]
    if emit_state:
        assert nb == 1
        out_shape.append(jax.ShapeDtypeStruct(s0.shape, F32))
        out_specs.append(pl.BlockSpec(s0.shape, lambda b, c: (0, 0, 0)))
    scratch = [pltpu.VMEM(s0.shape, F32), pltpu.VMEM((R_HEADS, T, T), F32)]
    return pl.pallas_call(
        functools.partial(_retention_kernel, T=T, nsub=nsub, emit_state=emit_state),
        grid=(nb, nc),
        in_specs=in_specs,
        out_specs=out_specs,
        out_shape=out_shape,
        scratch_shapes=scratch,
        compiler_params=pltpu.CompilerParams(
            dimension_semantics=("arbitrary", "arbitrary"),
            vmem_limit_bytes=VMEM_LIMIT_BYTES),
        name="retention_state" if emit_state else "retention",
    )(q, k, v, gs, gn, s0)


def _rms(x, g):
    return x * lax.rsqrt(jnp.mean(x * x, axis=-1, keepdims=True) + EPS) * g


def _out_ffn_kernel(hm_ref, hr_ref, ga_ref, gb_ref, x_ref, wpm_ref, wpr_ref, wout_ref,
                    wup_ref, wdn_ref, g1_ref, g2_ref, g3_ref, o_ref):
    ya = jnp.dot(hm_ref[...], wpm_ref[...], preferred_element_type=F32)
    yb = jnp.dot(hr_ref[...], wpr_ref[...], preferred_element_type=F32)
    mixin = ga_ref[...].astype(F32) * ya + gb_ref[...].astype(F32) * yb
    mix = jnp.dot(mixin.astype(BF16), wout_ref[...], preferred_element_type=F32)
    h1 = x_ref[...] + _rms(mix, g1_ref[...])
    u = _rms(h1, g2_ref[...]).astype(BF16)
    f = None
    for j in range(D_FF // FF_TILE):
        sl = slice(j * FF_TILE, (j + 1) * FF_TILE)
        a = jnp.maximum(jnp.dot(u, wup_ref[:, sl], preferred_element_type=F32), 0.0)
        part = jnp.dot((a * a).astype(BF16), wdn_ref[sl, :], preferred_element_type=F32)
        f = part if f is None else f + part
    o_ref[...] = h1 + _rms(f, g3_ref[...])


def _out_ffn(hm, hr, ga, gb, x2d, wpm, wpr, wout, wup, wdn, g1, g2, g3, tm):
    rows = x2d.shape[0]
    row_spec = lambda w: pl.BlockSpec((tm, w), lambda i: (i, 0))
    consts = (wpm, wpr, wout, wup, wdn, g1, g2, g3)
    return pl.pallas_call(
        _out_ffn_kernel,
        grid=(rows // tm,),
        in_specs=[row_spec(M_INNER), row_spec(R_V), row_spec(D_MODEL), row_spec(D_MODEL),
                  row_spec(D_MODEL), *[_const_spec(c.shape) for c in consts]],
        out_specs=row_spec(D_MODEL),
        out_shape=jax.ShapeDtypeStruct((rows, D_MODEL), F32),
        compiler_params=pltpu.CompilerParams(
            dimension_semantics=("arbitrary",),
            vmem_limit_bytes=VMEM_LIMIT_BYTES),
        name="out_ffn",
    )(hm, hr, ga, gb, x2d, *consts)


def _block_diag_tiles(w):
    rows = w.reshape(M_INNER, M_QKV_BLOCK)
    wide = jnp.tile(rows, (1, BD_TILE // M_QKV_BLOCK))
    r_blk = (lax.broadcasted_iota(jnp.int32, wide.shape, 0) % BD_TILE) // M_QKV_BLOCK
    c_blk = lax.broadcasted_iota(jnp.int32, wide.shape, 1) // M_QKV_BLOCK
    t = jnp.where(r_blk == c_blk, wide, 0.0).astype(BF16)
    return t.reshape(M_INNER // BD_TILE, BD_TILE, BD_TILE)


def _rope_tables(pos0, n):
    half = R_QK_DIM // 2
    inv = ROPE_BASE ** (-np.arange(half, dtype=np.float64) / half)
    ang = (pos0 + np.arange(n, dtype=np.float64))[:, None] * inv[None, :]
    return jnp.asarray(np.cos(ang), F32), jnp.asarray(np.sin(ang), F32)


def kernel(x, meta_tokens, norm_mix_pre, w_in, conv_w, conv_b, w_q_m, w_k_m, w_v_m, w_if, b_if,
           skip_m, gn_m, gn_r, w_proj_m, w_proj_r, w_out, norm_mix_post, norm_ffn_pre, w_up,
           w_down, norm_ffn_post):
    nb, seq, _ = x.shape
    assert norm_mix_pre.shape[0] == 1, "single-layer block"
    assert seq % PROJ_TM_WIDE == 0 and seq % CHUNK_T == 0
    row = lambda a: a[0].reshape(1, -1).astype(F32)
    n_pad = META_T - N_META

    w_in_bf = w_in[0].astype(BF16)
    g_pre = row(norm_mix_pre)
    cw, cb = conv_w[0].astype(F32), row(conv_b)

    meta_rows = jnp.concatenate(
        [jnp.zeros((n_pad, D_MODEL), x.dtype), meta_tokens.astype(x.dtype)], axis=0)
    cos_m, sin_m = _rope_tables(-n_pad, META_T)
    cos_x, sin_x = _rope_tables(N_META, seq)
    tail_zero = jnp.zeros((SUBLANES, M_INNER), F32)
    m_meta, r_meta, _ = _in_proj_all(meta_rows, g_pre, w_in_bf, cw, cb, tail_zero, cos_m, sin_m,
                                     META_T, 1, n_pad)
    x2d = x.reshape(nb * seq, D_MODEL)
    m_in, r_in, (ga, gb) = _in_proj_all(x2d, g_pre, w_in_bf, cw, cb, m_meta[3], cos_x, sin_x,
                                        PROJ_TM, seq // PROJ_TM, 0)

    nblk = M_INNER // M_QKV_BLOCK
    wif3 = w_if[0].reshape(3, nblk, M_QKV_BLOCK, 2 * M_HEADS)
    fold = lambda w, part: jnp.einsum('nio,noj->nij', w, part).reshape(M_INNER, 2 * M_HEADS)
    wg = jnp.concatenate([fold(w_q_m[0], wif3[0]) + fold(w_k_m[0], wif3[1]), fold(w_v_m[0], wif3[2])],
                         axis=0)
    wif = jnp.zeros((2 * M_INNER, GATE_PAD), F32).at[:, :2 * M_HEADS].set(wg).astype(BF16)
    bif = jnp.zeros((1, LANES), F32).at[0, :2 * M_HEADS].set(b_if[0])
    p_weights = (_block_diag_tiles(w_q_m[0]), _block_diag_tiles(w_k_m[0]),
                 _block_diag_tiles(w_v_m[0]), wif, bif)
    m_zero = (jnp.zeros((M_HEADS, M_HEAD_DIM, M_HEAD_DIM), F32),
              jnp.zeros((M_HEADS, SUBLANES, M_HEAD_DIM), F32),
              jnp.zeros((M_HEADS, SUBLANES, LANES), F32))
    as_seq = lambda a, n, t: a.reshape(n, t, a.shape[-1])
    xm_m, xc_m, zs_m = [as_seq(a, 1, META_T) for a in m_meta[:3]]
    xm_x, xc_x, zs_x = [as_seq(a, nb, seq) for a in m_in[:3]]
    prep_m = _mlstm_prep(xm_m, xc_m, p_weights, META_T, META_T, n_pad)
    m_init = _mlstm(*prep_m[:3], xc_m, zs_m, *prep_m[3:], row(skip_m), row(gn_m), m_zero,
                    META_T, True)[1:]
    prep_x = _mlstm_prep(xm_x, xc_x, p_weights, PREP_TP, CHUNK_T, 0)
    hm = _mlstm(*prep_x[:3], xc_x, zs_x, *prep_x[3:], row(skip_m), row(gn_m), tuple(m_init),
                CHUNK_T, False)[0]

    r_zero = jnp.zeros((R_HEADS, R_QK_DIM, R_V_DIM), F32)
    r_init = _retention(*[as_seq(a, 1, META_T) for a in r_meta], row(gn_r), r_zero,
                        META_T, True)[1]
    hr = _retention(*[as_seq(a, nb, seq) for a in r_in], row(gn_r), r_init, CHUNK_T, False)[0]

    out = _out_ffn(hm.reshape(nb * seq, M_INNER), hr.reshape(nb * seq, R_V), ga, gb, x2d,
                   w_proj_m[0].astype(BF16), w_proj_r[0].astype(BF16), w_out[0].astype(BF16),
                   w_up[0].astype(BF16), w_down[0].astype(BF16),
                   row(norm_mix_post), row(norm_ffn_pre), row(norm_ffn_post), OUT_TM)
    return out.reshape(nb, seq, D_MODEL)
```

```python
import functools
import math

import jax
import jax.numpy as jnp
import numpy as np
from jax import lax
from jax.experimental import pallas as pl
from jax.experimental.pallas import tpu as pltpu

F32 = jnp.float32
BF16 = jnp.bfloat16

D_MODEL = 1024
N_META = 16
M_INNER = 2 * D_MODEL
M_HEADS = 4
M_HEAD_DIM = M_INNER // M_HEADS
M_QKV_BLOCK = 4
M_CONV = 4
R_HEADS = 4
R_QK_DIM = D_MODEL // R_HEADS
R_V_DIM = 2 * R_QK_DIM
R_QK = R_HEADS * R_QK_DIM
R_V = R_HEADS * R_V_DIM
D_FF = 4 * D_MODEL
ROPE_BASE = 10000.0
EPS = 1e-6
NEG = -1e30
LOG2E = math.log2(math.e)
N_IN = 2 * M_INNER + 2 * R_QK + 2 * R_V + 2 * D_MODEL

LANES = 128
SUBLANES = 8
MXU_DIM = 256
VMEM_LIMIT_BYTES = 56 * 1024 * 1024

CHUNK_T = 256
META_T = 128
BD_TILE = MXU_DIM
GATE_PAD = MXU_DIM
MLSTM_STEP_CHUNKS = 1
RET_STEP_CHUNKS = 4
PREP_TP = 512
CONV_STRIP = 256
PROJ_TM = 512
PROJ_TM_WIDE = 1024
OUT_TM = 512
FF_TILE = 1024

_NT = (((1,), (1,)), ((), ()))
_TN = (((0,), (0,)), ((), ()))


def _const_spec(shape):
    nd = len(shape)
    return pl.BlockSpec(shape, lambda *_: (0,) * nd, pipeline_mode=pl.Buffered(1))


def _sigmoid(x):
    return 1.0 / (1.0 + jnp.exp2(x * (-LOG2E)))


def _silu(x):
    return x * _sigmoid(x)


def _normed(x_ref, g_ref):
    x = x_ref[...]
    ms = jnp.mean(x * x, axis=-1, keepdims=True)
    return (x * lax.rsqrt(ms + EPS) * g_ref[...]).astype(BF16)


def _inproj_mlstm_kernel(x_ref, g_ref, w_ref, cw_ref, cb_ref, t0_ref,
                         xm_ref, xc_ref, zs_ref, to_ref, halo_s, res_s, *, tm, tiles_per_seq):
    i = pl.program_id(0)

    @pl.when(i % tiles_per_seq == 0)
    def _():
        halo_s[...] = t0_ref[...]

    u = _normed(x_ref, g_ref)
    nstrip = M_INNER // CONV_STRIP
    nv = tm // SUBLANES
    sub = lax.broadcasted_iota(jnp.int32, (nv, SUBLANES, CONV_STRIP), 1)

    def strip_dot(c):
        sl = slice(c * CONV_STRIP, (c + 1) * CONV_STRIP)
        res_s[(i + c) % 2] = jnp.dot(u, w_ref[:, sl], preferred_element_type=F32)

    strip_dot(0)
    for c in range(nstrip):
        sl = slice(c * CONV_STRIP, (c + 1) * CONV_STRIP)
        if c + 1 < nstrip:
            strip_dot(c + 1)
        zsl = slice(M_INNER + c * CONV_STRIP, M_INNER + (c + 1) * CONV_STRIP)
        zm = jnp.dot(u, w_ref[:, zsl], preferred_element_type=F32)
        zs_ref[:, sl] = _silu(zm).astype(BF16)
        xm = res_s[(i + c) % 2]
        xm_ref[:, sl] = xm.astype(BF16)
        x3 = xm.reshape(nv, SUBLANES, CONV_STRIP)
        prev3 = jnp.concatenate([halo_s[:, sl].reshape(1, SUBLANES, CONV_STRIP), x3[:nv - 1]], axis=0)
        y = cb_ref[:, sl] + cw_ref[M_CONV - 1:M_CONV, sl] * x3
        for j in range(M_CONV - 1):
            sh = M_CONV - 1 - j
            merged = jnp.where(sub >= SUBLANES - sh, prev3, x3)
            y = y + cw_ref[j:j + 1, sl] * pltpu.roll(merged, sh, axis=1)
        halo_s[:, sl] = xm[tm - SUBLANES:, :]
        xc_ref[:, sl] = _silu(y.reshape(tm, CONV_STRIP)).astype(BF16)

    @pl.when(i == pl.num_programs(0) - 1)
    def _():
        to_ref[...] = halo_s[...]


def _inproj_rot_gate_kernel(x_ref, g_ref, w_ref, wmix_ref, cos_ref, sin_ref,
                            q_ref, k_ref, ga_ref, gb_ref, *, tm, n_pad):
    u = _normed(x_ref, g_ref)
    cos = cos_ref[...]
    sin = sin_ref[...]
    half = R_QK_DIM // 2
    kscale = R_QK_DIM ** -0.5
    if n_pad:
        kmul = (lax.broadcasted_iota(jnp.int32, (tm, 1), 0) >= n_pad).astype(F32) * kscale

    def rot(t, h):
        t1 = t[:, h * R_QK_DIM:h * R_QK_DIM + half]
        t2 = t[:, h * R_QK_DIM + half:(h + 1) * R_QK_DIM]
        return t1 * cos - t2 * sin, t1 * sin + t2 * cos

    q = jnp.dot(u, w_ref[:, 0:R_QK], preferred_element_type=F32)
    for h in range(R_HEADS):
        a, b = rot(q, h)
        q_ref[:, h * R_QK_DIM:h * R_QK_DIM + half] = a.astype(BF16)
        q_ref[:, h * R_QK_DIM + half:(h + 1) * R_QK_DIM] = b.astype(BF16)
    k = jnp.dot(u, w_ref[:, R_QK:2 * R_QK], preferred_element_type=F32)
    for h in range(R_HEADS):
        a, b = rot(k, h)
        if n_pad:
            a, b = a * kmul, b * kmul
        else:
            a, b = a * kscale, b * kscale
        k_ref[:, h * R_QK_DIM:h * R_QK_DIM + half] = a.astype(BF16)
        k_ref[:, h * R_QK_DIM + half:(h + 1) * R_QK_DIM] = b.astype(BF16)
    ga_ref[...] = _sigmoid(jnp.dot(u, wmix_ref[:, 0:D_MODEL], preferred_element_type=F32)).astype(BF16)
    gb_ref[...] = _sigmoid(
        jnp.dot(u, wmix_ref[:, D_MODEL:2 * D_MODEL], preferred_element_type=F32)).astype(BF16)


def _inproj_vg_kernel(x_ref, g_ref, wv_ref, wg_ref, v_ref, gs_ref):
    u = _normed(x_ref, g_ref)
    v_ref[...] = jnp.dot(u, wv_ref[...], preferred_element_type=F32).astype(BF16)
    gs_ref[...] = _silu(jnp.dot(u, wg_ref[...], preferred_element_type=F32)).astype(BF16)


def _w_cols(width, start):
    assert start % width == 0
    return pl.BlockSpec((D_MODEL, width), lambda *_: (0, start // width), pipeline_mode=pl.Buffered(1))


def _in_proj_call(body, name, x2d, gain, w_bf, w_specs, extra_in, extra_specs, out_widths, tm,
                  extra_out=(), extra_out_specs=(), scratch=()):
    rows = x2d.shape[0]
    row_spec = lambda w: pl.BlockSpec((tm, w), lambda i: (i, 0))
    return pl.pallas_call(
        body,
        grid=(rows // tm,),
        in_specs=[row_spec(D_MODEL), _const_spec(gain.shape), *w_specs, *extra_specs],
        out_specs=[*[row_spec(w) for w in out_widths], *extra_out_specs],
        out_shape=[*[jax.ShapeDtypeStruct((rows, w), BF16) for w in out_widths], *extra_out],
        scratch_shapes=list(scratch),
        compiler_params=pltpu.CompilerParams(
            dimension_semantics=("arbitrary",),
            vmem_limit_bytes=VMEM_LIMIT_BYTES),
        name=name,
    )(x2d, gain, *([w_bf] * len(w_specs)), *extra_in)


def _in_proj_all(x2d, gain, w_bf, cw, cb, tail0, cos, sin, tm, tiles_per_seq, n_pad):
    c_qk = 2 * M_INNER
    c_v = c_qk + 2 * R_QK
    c_g = c_v + R_V
    c_mix = c_g + R_V
    tail_shape = (SUBLANES, M_INNER)
    xm, xc, zs, tail = _in_proj_call(
        functools.partial(_inproj_mlstm_kernel, tm=tm, tiles_per_seq=tiles_per_seq),
        "in_proj_mlstm", x2d, gain, w_bf, [_w_cols(2 * M_INNER, 0)], (cw, cb, tail0),
        (_const_spec(cw.shape), _const_spec(cb.shape), _const_spec(tail_shape)),
        (M_INNER, M_INNER, M_INNER), tm,
        extra_out=(jax.ShapeDtypeStruct(tail_shape, F32),),
        extra_out_specs=(pl.BlockSpec(tail_shape, lambda i: (0, 0)),),
        scratch=(pltpu.VMEM(tail_shape, F32), pltpu.VMEM((2, tm, CONV_STRIP), F32)))
    half = R_QK_DIM // 2
    tmw = min(tm * PROJ_TM_WIDE // PROJ_TM, x2d.shape[0])
    tps_w = tiles_per_seq * tm // tmw
    pos_spec = pl.BlockSpec((tmw, half), lambda i: (i % tps_w, 0))
    qr, kr, ga, gb = _in_proj_call(
        functools.partial(_inproj_rot_gate_kernel, tm=tmw, n_pad=n_pad),
        "in_proj_rot_gate", x2d, gain, w_bf, [_w_cols(2 * R_QK, c_qk), _w_cols(2 * D_MODEL, c_mix)],
        (cos, sin), (pos_spec, pos_spec), (R_QK, R_QK, D_MODEL, D_MODEL), tmw)
    v, gs = _in_proj_call(_inproj_vg_kernel, "in_proj_vg", x2d, gain, w_bf,
                          [_w_cols(R_V, c_v), _w_cols(R_V, c_g)], (), (), (R_V, R_V), tmw)
    return (xm, xc, zs, tail), (qr, kr, v, gs), (ga, gb)


def _head_norm(h, w):
    mu = jnp.mean(h, axis=-1, keepdims=True)
    c = h - mu
    var = jnp.mean(c * c, axis=-1, keepdims=True)
    return c * lax.rsqrt(var + EPS) * w


def _lane_cumsum(x, n):
    lane = lax.broadcasted_iota(jnp.int32, x.shape, 1)
    d = 1
    while d < n:
        x = x + jnp.where(lane >= d, pltpu.roll(x, d, axis=1), 0.0)
        d *= 2
    return x


def _mlstm_prep_kernel(xm_ref, xc_ref, bdq_ref, bdk_ref, bdv_ref, wif_ref, bif_ref,
                       q_ref, k_ref, v_ref, row_ref, col_ref, *, TP, T, n_pad):
    scale = M_HEAD_DIM ** -0.5
    for j in range(M_INNER // BD_TILE):
        sl = slice(j * BD_TILE, (j + 1) * BD_TILE)
        q_ref[:, sl] = jnp.dot(xc_ref[:, sl], bdq_ref[j], preferred_element_type=F32).astype(BF16)
        kj = jnp.dot(xc_ref[:, sl], bdk_ref[j], preferred_element_type=F32)
        k_ref[:, sl] = (kj * scale).astype(BF16)
        v_ref[:, sl] = jnp.dot(xm_ref[:, sl], bdv_ref[j], preferred_element_type=F32).astype(BF16)
    xcm = jnp.concatenate([xc_ref[...], xm_ref[...]], axis=1)
    g_all = jnp.dot(xcm, wif_ref[...], preferred_element_type=F32)[:, 0:LANES] + bif_ref[...]
    lane = lax.broadcasted_iota(jnp.int32, (T, LANES), 1)
    subl = lax.broadcasted_iota(jnp.int32, (SUBLANES, T), 0)
    for c in range(TP // T):
        g = g_all[c * T:(c + 1) * T, :]
        if n_pad:
            rowv = lax.broadcasted_iota(jnp.int32, (T, LANES), 0) >= n_pad
            g = jnp.where(jnp.logical_and(lane < M_HEADS, jnp.logical_not(rowv)), NEG, g)
        gt = g.T
        lf_t = jnp.minimum(gt, 0.0) - jnp.log1p(jnp.exp(-jnp.abs(gt)))
        cum_t = _lane_cumsum(lf_t, T)
        row_ref[c] = jnp.where(subl < M_HEADS, gt[0:SUBLANES, :], cum_t[0:SUBLANES, :])
        col_ref[c * T:(c + 1) * T, :] = jnp.where(lane < M_HEADS, g, cum_t.T)


def _mlstm_prep(xm, xc, weights, TP, T, n_pad):
    nb, L, _ = xm.shape
    blk = pl.BlockSpec((None, TP, M_INNER), lambda b, i: (b, i, 0))
    big = jax.ShapeDtypeStruct((nb, L, M_INNER), BF16)
    return pl.pallas_call(
        functools.partial(_mlstm_prep_kernel, TP=TP, T=T, n_pad=n_pad),
        grid=(nb, L // TP),
        in_specs=[blk, blk, *[_const_spec(w.shape) for w in weights]],
        out_specs=[blk, blk, blk,
                   pl.BlockSpec((None, TP // T, SUBLANES, T), lambda b, i: (b, i, 0, 0)),
                   pl.BlockSpec((None, TP, LANES), lambda b, i: (b, i, 0))],
        out_shape=[big, big, big,
                   jax.ShapeDtypeStruct((nb, L // T, SUBLANES, T), F32),
                   jax.ShapeDtypeStruct((nb, L, LANES), F32)],
        compiler_params=pltpu.CompilerParams(
            dimension_semantics=("arbitrary", "arbitrary"),
            vmem_limit_bytes=VMEM_LIMIT_BYTES),
        name="mlstm_prep",
    )(xm, xc, *weights)


def _mlstm_kernel(q_ref, k_ref, v_ref, xc_ref, zs_ref, row_ref, col_ref, skip_ref, gn_ref,
                  c0_ref, n0_ref, m0_ref, h_ref, *rest, T, nsub, emit_state):
    if emit_state:
        co_ref, no_ref, mo_ref, c_s, cb_s, n_s, m_s = rest
    else:
        c_s, cb_s, n_s, m_s = rest
    ci = pl.program_id(1)
    nc = pl.num_programs(1)

    @pl.when(ci == 0)
    def _():
        c_s[...] = c0_ref[...]
        cb_s[...] = c0_ref[...].astype(BF16)
        n_s[...] = n0_ref[...]
        m_s[...] = m0_ref[...]

    causal = (lax.broadcasted_iota(jnp.int32, (T, T), 0)
              >= lax.broadcasted_iota(jnp.int32, (T, T), 1))
    heads = range(M_HEADS)
    sls = [slice(h * M_HEAD_DIM, (h + 1) * M_HEAD_DIM) for h in heads]

    def chunk(rows, rowp):
        colp = col_ref[rows, :]

        s_raw = [lax.dot_general(q_ref[rows, sls[h]], k_ref[rows, sls[h]], _NT,
                                 preferred_element_type=F32) for h in heads]
        qc = [jnp.dot(q_ref[rows, sls[h]], cb_s[h], preferred_element_type=F32) for h in heads]

        dmat, w_inter, e_negm, w_k, decay, m_new = [], [], [], [], [], []
        for h in heads:
            i_row = rowp[h:h + 1, :]
            f_row = rowp[M_HEADS + h:M_HEADS + h + 1, :]
            i_col = colp[:, h:h + 1]
            f_col = colp[:, M_HEADS + h:M_HEADS + h + 1]
            g_tot = f_row[:, T - 1:T]
            m_prev = m_s[h][0:1, 0:1]
            b_row = i_row - f_row
            log_d = jnp.where(causal, f_col + b_row, NEG)
            inter = f_col + m_prev
            m_row = jnp.maximum(inter, jnp.max(log_d, axis=1, keepdims=True))
            dmat.append(jnp.exp(log_d - m_row))
            w_inter.append(jnp.exp(inter - m_row))
            e_negm.append(jnp.exp(-m_row))
            mn = jnp.maximum(g_tot + m_prev, jnp.max(g_tot + b_row, axis=1, keepdims=True))
            m_new.append(mn)
            w_k.append(jnp.exp(g_tot - f_col + i_col - mn))
            decay.append(jnp.exp(g_tot + m_prev - mn))

        hhs = []
        for h in heads:
            qn = jnp.sum(q_ref[rows, sls[h]].astype(F32) * n_s[h][0:1, :], axis=1, keepdims=True)
            s = s_raw[h] * dmat[h]
            num = (jnp.dot(s.astype(BF16), v_ref[rows, sls[h]], preferred_element_type=F32)
                   + w_inter[h] * qc[h])
            den = jnp.sum(s, axis=1, keepdims=True) + w_inter[h] * qn
            hhs.append(num * (1.0 / jnp.maximum(jnp.abs(den), e_negm[h])))
        for h in heads:
            kw = k_ref[rows, sls[h]].astype(F32) * w_k[h]
            c_new = decay[h] * c_s[h] + lax.dot_general(kw.astype(BF16), v_ref[rows, sls[h]], _TN,
                                                        preferred_element_type=F32)
            n_new = decay[h] * n_s[h][0:1, :] + jnp.sum(kw, axis=0, keepdims=True)
            c_s[h] = c_new
            cb_s[h] = c_new.astype(BF16)
            n_s[h] = jnp.broadcast_to(n_new, (SUBLANES, M_HEAD_DIM))
            m_s[h] = jnp.broadcast_to(m_new[h], (SUBLANES, LANES))
        for h in heads:
            hn = _head_norm(hhs[h], gn_ref[:, sls[h]])
            out = ((hn + skip_ref[:, sls[h]] * xc_ref[rows, sls[h]].astype(F32))
                   * zs_ref[rows, sls[h]].astype(F32))
            h_ref[rows, sls[h]] = out.astype(BF16)

    for cc in range(nsub):
        chunk(slice(cc * T, (cc + 1) * T), row_ref[cc])

    if emit_state:
        @pl.when(ci == nc - 1)
        def _():
            co_ref[...] = c_s[...]
            no_ref[...] = n_s[...]
            mo_ref[...] = m_s[...]


def _mlstm(q, k, v, xc, zs, rowp, colp, skip, gn, state, T, emit_state):
    nb, L, _ = q.shape
    nsub = min(MLSTM_STEP_CHUNKS, L // T)
    nc = L // (T * nsub)
    blk = pl.BlockSpec((None, T * nsub, M_INNER), lambda b, c: (b, c, 0))
    in_specs = [blk, blk, blk, blk, blk,
                pl.BlockSpec((None, nsub, SUBLANES, T), lambda b, c: (b, c, 0, 0)),
                pl.BlockSpec((None, T * nsub, LANES), lambda b, c: (b, c, 0)),
                _const_spec(skip.shape), _const_spec(gn.shape),
                *[_const_spec(s.shape) for s in state]]
    out_shape = [jax.ShapeDtypeStruct((nb, L, M_INNER), BF16)]
    out_specs = [blk]
    if emit_state:
        assert nb == 1
        for s in state:
            out_shape.append(jax.ShapeDtypeStruct(s.shape, F32))
            out_specs.append(pl.BlockSpec(s.shape, lambda b, c, _n=len(s.shape): (0,) * _n))
    c_shape = state[0].shape
    scratch = [pltpu.VMEM(c_shape, F32), pltpu.VMEM(c_shape, BF16),
               pltpu.VMEM(state[1].shape, F32), pltpu.VMEM(state[2].shape, F32)]
    return pl.pallas_call(
        functools.partial(_mlstm_kernel, T=T, nsub=nsub, emit_state=emit_state),
        grid=(nb, nc),
        in_specs=in_specs,
        out_specs=out_specs,
        out_shape=out_shape,
        scratch_shapes=scratch,
        compiler_params=pltpu.CompilerParams(
            dimension_semantics=("arbitrary", "arbitrary"),
            vmem_limit_bytes=VMEM_LIMIT_BYTES),
        name="mlstm_state" if emit_state else "mlstm",
    )(q, k, v, xc, zs, rowp, colp, skip, gn, *state)


def _retention_kernel(q_ref, k_ref, v_ref, gs_ref, gn_ref, s0_ref,
                      o_ref, *rest, T, nsub, emit_state):
    if emit_state:
        so_ref, st_s, dm_s = rest
    else:
        st_s, dm_s = rest
    ci = pl.program_id(1)
    nc = pl.num_programs(1)
    log_gamma = [math.log(1.0 - 2.0 ** (-5.0 - h)) for h in range(R_HEADS)]

    @pl.when(ci == 0)
    def _():
        st_s[...] = s0_ref[...]
        diff = (lax.broadcasted_iota(jnp.int32, (T, T), 0)
                - lax.broadcasted_iota(jnp.int32, (T, T), 1))
        dfl = diff.astype(F32)
        for h in range(R_HEADS):
            dm_s[h] = jnp.where(diff >= 0, jnp.exp(log_gamma[h] * dfl), 0.0)

    tf = lax.broadcasted_iota(jnp.int32, (T, 1), 0).astype(F32)
    for cc in range(nsub):
        rows = slice(cc * T, (cc + 1) * T)
        for h in range(R_HEADS):
            qh = q_ref[rows, h * R_QK_DIM:(h + 1) * R_QK_DIM]
            kh = k_ref[rows, h * R_QK_DIM:(h + 1) * R_QK_DIM]
            sl = slice(h * R_V_DIM, (h + 1) * R_V_DIM)
            vh = v_ref[rows, sl]
            s = lax.dot_general(qh, kh, _NT, preferred_element_type=F32) * dm_s[h]
            q_decay = jnp.exp(log_gamma[h] * (tf + 1.0))
            k_decay = jnp.exp(log_gamma[h] * (T - 1.0 - tf))
            o = (jnp.dot(s.astype(BF16), vh, preferred_element_type=F32)
                 + q_decay * jnp.dot(qh, st_s[h].astype(BF16), preferred_element_type=F32))
            kd = (kh.astype(F32) * k_decay).astype(BF16)
            st_s[h] = (math.exp(log_gamma[h] * T) * st_s[h]
                       + lax.dot_general(kd, vh, _TN, preferred_element_type=F32))
            o_ref[rows, sl] = (_head_norm(o, gn_ref[:, sl])
                               * gs_ref[rows, sl].astype(F32)).astype(BF16)

    if emit_state:
        @pl.when(ci == nc - 1)
        def _():
            so_ref[...] = st_s[...]


def _retention(q, k, v, gs, gn, s0, T, emit_state):
    nb, L, _ = q.shape
    nsub = min(RET_STEP_CHUNKS, L // T)
    nc = L // (T * nsub)
    qk_blk = pl.BlockSpec((None, T * nsub, R_QK), lambda b, c: (b, c, 0))
    v_blk = pl.BlockSpec((None, T * nsub, R_V), lambda b, c: (b, c, 0))
    in_specs = [qk_blk, qk_blk, v_blk, v_blk, _const_spec(gn.shape), _const_spec(s0.shape)]
    out_shape = [jax.ShapeDtypeStruct((nb, L, R_V), BF16)]
    out_specs = [v_blk]
    if emit_state:
        assert nb == 1
        out_shape.append(jax.ShapeDtypeStruct(s0.shape, F32))
        out_specs.append(pl.BlockSpec(s0.shape, lambda b, c: (0, 0, 0)))
    scratch = [pltpu.VMEM(s0.shape, F32), pltpu.VMEM((R_HEADS, T, T), F32)]
    return pl.pallas_call(
        functools.partial(_retention_kernel, T=T, nsub=nsub, emit_state=emit_state),
        grid=(nb, nc),
        in_specs=in_specs,
        out_specs=out_specs,
        out_shape=out_shape,
        scratch_shapes=scratch,
        compiler_params=pltpu.CompilerParams(
            dimension_semantics=("arbitrary", "arbitrary"),
            vmem_limit_bytes=VMEM_LIMIT_BYTES),
        name="retention_state" if emit_state else "retention",
    )(q, k, v, gs, gn, s0)


def _rms(x, g):
    return x * lax.rsqrt(jnp.mean(x * x, axis=-1, keepdims=True) + EPS) * g


def _out_ffn_kernel(hm_ref, hr_ref, ga_ref, gb_ref, x_ref, wpm_ref, wpr_ref, wout_ref,
                    wup_ref, wdn_ref, g1_ref, g2_ref, g3_ref, o_ref):
    ya = jnp.dot(hm_ref[...], wpm_ref[...], preferred_element_type=F32)
    yb = jnp.dot(hr_ref[...], wpr_ref[...], preferred_element_type=F32)
    mixin = ga_ref[...].astype(F32) * ya + gb_ref[...].astype(F32) * yb
    mix = jnp.dot(mixin.astype(BF16), wout_ref[...], preferred_element_type=F32)
    h1 = x_ref[...] + _rms(mix, g1_ref[...])
    u = _rms(h1, g2_ref[...]).astype(BF16)
    f = None
    for j in range(D_FF // FF_TILE):
        sl = slice(j * FF_TILE, (j + 1) * FF_TILE)
        a = jnp.maximum(jnp.dot(u, wup_ref[:, sl], preferred_element_type=F32), 0.0)
        part = jnp.dot((a * a).astype(BF16), wdn_ref[sl, :], preferred_element_type=F32)
        f = part if f is None else f + part
    o_ref[...] = h1 + _rms(f, g3_ref[...])


def _out_ffn(hm, hr, ga, gb, x2d, wpm, wpr, wout, wup, wdn, g1, g2, g3, tm):
    rows = x2d.shape[0]
    row_spec = lambda w: pl.BlockSpec((tm, w), lambda i: (i, 0))
    consts = (wpm, wpr, wout, wup, wdn, g1, g2, g3)
    return pl.pallas_call(
        _out_ffn_kernel,
        grid=(rows // tm,),
        in_specs=[row_spec(M_INNER), row_spec(R_V), row_spec(D_MODEL), row_spec(D_MODEL),
                  row_spec(D_MODEL), *[_const_spec(c.shape) for c in consts]],
        out_specs=row_spec(D_MODEL),
        out_shape=jax.ShapeDtypeStruct((rows, D_MODEL), F32),
        compiler_params=pltpu.CompilerParams(
            dimension_semantics=("arbitrary",),
            vmem_limit_bytes=VMEM_LIMIT_BYTES),
        name="out_ffn",
    )(hm, hr, ga, gb, x2d, *consts)


def _block_diag_tiles(w):
    rows = w.reshape(M_INNER, M_QKV_BLOCK)
    wide = jnp.tile(rows, (1, BD_TILE // M_QKV_BLOCK))
    r_blk = (lax.broadcasted_iota(jnp.int32, wide.shape, 0) % BD_TILE) // M_QKV_BLOCK
    c_blk = lax.broadcasted_iota(jnp.int32, wide.shape, 1) // M_QKV_BLOCK
    t = jnp.where(r_blk == c_blk, wide, 0.0).astype(BF16)
    return t.reshape(M_INNER // BD_TILE, BD_TILE, BD_TILE)


def _rope_tables(pos0, n):
    half = R_QK_DIM // 2
    inv = ROPE_BASE ** (-np.arange(half, dtype=np.float64) / half)
    ang = (pos0 + np.arange(n, dtype=np.float64))[:, None] * inv[None, :]
    return jnp.asarray(np.cos(ang), F32), jnp.asarray(np.sin(ang), F32)


def kernel(x, meta_tokens, norm_mix_pre, w_in, conv_w, conv_b, w_q_m, w_k_m, w_v_m, w_if, b_if,
           skip_m, gn_m, gn_r, w_proj_m, w_proj_r, w_out, norm_mix_post, norm_ffn_pre, w_up,
           w_down, norm_ffn_post):
    nb, seq, _ = x.shape
    assert norm_mix_pre.shape[0] == 1, "single-layer block"
    assert seq % PROJ_TM_WIDE == 0 and seq % CHUNK_T == 0
    row = lambda a: a[0].reshape(1, -1).astype(F32)
    n_pad = META_T - N_META

    w_in_bf = w_in[0].astype(BF16)
    g_pre = row(norm_mix_pre)
    cw, cb = conv_w[0].astype(F32), row(conv_b)

    meta_rows = jnp.concatenate(
        [jnp.zeros((n_pad, D_MODEL), x.dtype), meta_tokens.astype(x.dtype)], axis=0)
    cos_m, sin_m = _rope_tables(-n_pad, META_T)
    cos_x, sin_x = _rope_tables(N_META, seq)
    tail_zero = jnp.zeros((SUBLANES, M_INNER), F32)
    m_meta, r_meta, _ = _in_proj_all(meta_rows, g_pre, w_in_bf, cw, cb, tail_zero, cos_m, sin_m,
                                     META_T, 1, n_pad)
    x2d = x.reshape(nb * seq, D_MODEL)
    m_in, r_in, (ga, gb) = _in_proj_all(x2d, g_pre, w_in_bf, cw, cb, m_meta[3], cos_x, sin_x,
                                        PROJ_TM, seq // PROJ_TM, 0)

    nblk = M_INNER // M_QKV_BLOCK
    wif3 = w_if[0].reshape(3, nblk, M_QKV_BLOCK, 2 * M_HEADS)
    fold = lambda w, part: jnp.einsum('nio,noj->nij', w, part).reshape(M_INNER, 2 * M_HEADS)
    wg = jnp.concatenate([fold(w_q_m[0], wif3[0]) + fold(w_k_m[0], wif3[1]), fold(w_v_m[0], wif3[2])],
                         axis=0)
    wif = jnp.zeros((2 * M_INNER, GATE_PAD), F32).at[:, :2 * M_HEADS].set(wg).astype(BF16)
    bif = jnp.zeros((1, LANES), F32).at[0, :2 * M_HEADS].set(b_if[0])
    p_weights = (_block_diag_tiles(w_q_m[0]), _block_diag_tiles(w_k_m[0]),
                 _block_diag_tiles(w_v_m[0]), wif, bif)
    m_zero = (jnp.zeros((M_HEADS, M_HEAD_DIM, M_HEAD_DIM), F32),
              jnp.zeros((M_HEADS, SUBLANES, M_HEAD_DIM), F32),
              jnp.zeros((M_HEADS, SUBLANES, LANES), F32))
    as_seq = lambda a, n, t: a.reshape(n, t, a.shape[-1])
    xm_m, xc_m, zs_m = [as_seq(a, 1, META_T) for a in m_meta[:3]]
    xm_x, xc_x, zs_x = [as_seq(a, nb, seq) for a in m_in[:3]]
    prep_m = _mlstm_prep(xm_m, xc_m, p_weights, META_T, META_T, n_pad)
    m_init = _mlstm(*prep_m[:3], xc_m, zs_m, *prep_m[3:], row(skip_m), row(gn_m), m_zero,
                    META_T, True)[1:]
    prep_x = _mlstm_prep(xm_x, xc_x, p_weights, PREP_TP, CHUNK_T, 0)
    hm = _mlstm(*prep_x[:3], xc_x, zs_x, *prep_x[3:], row(skip_m), row(gn_m), tuple(m_init),
                CHUNK_T, False)[0]

    r_zero = jnp.zeros((R_HEADS, R_QK_DIM, R_V_DIM), F32)
    r_init = _retention(*[as_seq(a, 1, META_T) for a in r_meta], row(gn_r), r_zero,
                        META_T, True)[1]
    hr = _retention(*[as_seq(a, nb, seq) for a in r_in], row(gn_r), r_init, CHUNK_T, False)[0]

    out = _out_ffn(hm.reshape(nb * seq, M_INNER), hr.reshape(nb * seq, R_V), ga, gb, x2d,
                   w_proj_m[0].astype(BF16), w_proj_r[0].astype(BF16), w_out[0].astype(BF16),
                   w_up[0].astype(BF16), w_down[0].astype(BF16),
                   row(norm_mix_post), row(norm_ffn_pre), row(norm_ffn_post), OUT_TM)
    return out.reshape(nb, seq, D_MODEL)
```

```python
import functools
import math

import jax
import jax.numpy as jnp
import numpy as np
from jax import lax
from jax.experimental import pallas as pl
from jax.experimental.pallas import tpu as pltpu

F32 = jnp.float32
BF16 = jnp.bfloat16

D_MODEL = 1024
N_META = 16
M_INNER = 2 * D_MODEL
M_HEADS = 4
M_HEAD_DIM = M_INNER // M_HEADS
M_QKV_BLOCK = 4
M_CONV = 4
R_HEADS = 4
R_QK_DIM = D_MODEL // R_HEADS
R_V_DIM = 2 * R_QK_DIM
R_QK = R_HEADS * R_QK_DIM
R_V = R_HEADS * R_V_DIM
D_FF = 4 * D_MODEL
ROPE_BASE = 10000.0
EPS = 1e-6
NEG = -1e30
LOG2E = math.log2(math.e)
N_IN = 2 * M_INNER + 2 * R_QK + 2 * R_V + 2 * D_MODEL

LANES = 128
SUBLANES = 8
MXU_DIM = 256
VMEM_LIMIT_BYTES = 56 * 1024 * 1024

CHUNK_T = 256
META_T = 128
BD_TILE = MXU_DIM
GATE_PAD = MXU_DIM
MLSTM_STEP_CHUNKS = 1
RET_STEP_CHUNKS = 4
PREP_TP = 1024
CONV_STRIP = 256
PROJ_TM = 1024
OUT_TM = 512
FF_TILE = 1024

_NT = (((1,), (1,)), ((), ()))
_TN = (((0,), (0,)), ((), ()))


def _const_spec(shape):
    nd = len(shape)
    return pl.BlockSpec(shape, lambda *_: (0,) * nd, pipeline_mode=pl.Buffered(1))


def _sigmoid(x):
    return 1.0 / (1.0 + jnp.exp2(x * (-LOG2E)))


def _silu(x):
    return x * _sigmoid(x)


def _normed(x_ref, g_ref):
    x = x_ref[...]
    ms = jnp.mean(x * x, axis=-1, keepdims=True)
    return (x * lax.rsqrt(ms + EPS) * g_ref[...]).astype(BF16)


def _inproj_mlstm_kernel(x_ref, g_ref, w_ref, cw_ref, cb_ref, t0_ref,
                         xm_ref, xc_ref, zs_ref, to_ref, halo_s, res_s, *, tm, tiles_per_seq):
    i = pl.program_id(0)

    @pl.when(i % tiles_per_seq == 0)
    def _():
        halo_s[...] = t0_ref[...]

    u = _normed(x_ref, g_ref)
    nstrip = M_INNER // CONV_STRIP
    nv = tm // SUBLANES
    sub = lax.broadcasted_iota(jnp.int32, (nv, SUBLANES, CONV_STRIP), 1)

    def strip_dot(c):
        sl = slice(c * CONV_STRIP, (c + 1) * CONV_STRIP)
        res_s[(i + c) % 2] = jnp.dot(u, w_ref[:, sl], preferred_element_type=F32)

    strip_dot(0)
    for c in range(nstrip):
        sl = slice(c * CONV_STRIP, (c + 1) * CONV_STRIP)
        if c + 1 < nstrip:
            strip_dot(c + 1)
        zsl = slice(M_INNER + c * CONV_STRIP, M_INNER + (c + 1) * CONV_STRIP)
        zm = jnp.dot(u, w_ref[:, zsl], preferred_element_type=F32)
        zs_ref[:, sl] = _silu(zm).astype(BF16)
        xm = res_s[(i + c) % 2]
        xm_ref[:, sl] = xm.astype(BF16)
        x3 = xm.reshape(nv, SUBLANES, CONV_STRIP)
        prev3 = jnp.concatenate([halo_s[:, sl].reshape(1, SUBLANES, CONV_STRIP), x3[:nv - 1]], axis=0)
        y = cb_ref[:, sl] + cw_ref[M_CONV - 1:M_CONV, sl] * x3
        for j in range(M_CONV - 1):
            sh = M_CONV - 1 - j
            merged = jnp.where(sub >= SUBLANES - sh, prev3, x3)
            y = y + cw_ref[j:j + 1, sl] * pltpu.roll(merged, sh, axis=1)
        halo_s[:, sl] = xm[tm - SUBLANES:, :]
        xc_ref[:, sl] = _silu(y.reshape(tm, CONV_STRIP)).astype(BF16)

    @pl.when(i == pl.num_programs(0) - 1)
    def _():
        to_ref[...] = halo_s[...]


def _inproj_rot_gate_kernel(x_ref, g_ref, w_ref, wmix_ref, cos_ref, sin_ref,
                            q_ref, k_ref, ga_ref, gb_ref, *, tm, n_pad):
    u = _normed(x_ref, g_ref)
    cos = cos_ref[...]
    sin = sin_ref[...]
    half = R_QK_DIM // 2
    kscale = R_QK_DIM ** -0.5
    if n_pad:
        kmul = (lax.broadcasted_iota(jnp.int32, (tm, 1), 0) >= n_pad).astype(F32) * kscale

    def rot(t, h):
        t1 = t[:, h * R_QK_DIM:h * R_QK_DIM + half]
        t2 = t[:, h * R_QK_DIM + half:(h + 1) * R_QK_DIM]
        return t1 * cos - t2 * sin, t1 * sin + t2 * cos

    q = jnp.dot(u, w_ref[:, 0:R_QK], preferred_element_type=F32)
    for h in range(R_HEADS):
        a, b = rot(q, h)
        q_ref[:, h * R_QK_DIM:h * R_QK_DIM + half] = a.astype(BF16)
        q_ref[:, h * R_QK_DIM + half:(h + 1) * R_QK_DIM] = b.astype(BF16)
    k = jnp.dot(u, w_ref[:, R_QK:2 * R_QK], preferred_element_type=F32)
    for h in range(R_HEADS):
        a, b = rot(k, h)
        if n_pad:
            a, b = a * kmul, b * kmul
        else:
            a, b = a * kscale, b * kscale
        k_ref[:, h * R_QK_DIM:h * R_QK_DIM + half] = a.astype(BF16)
        k_ref[:, h * R_QK_DIM + half:(h + 1) * R_QK_DIM] = b.astype(BF16)
    ga_ref[...] = _sigmoid(jnp.dot(u, wmix_ref[:, 0:D_MODEL], preferred_element_type=F32)).astype(BF16)
    gb_ref[...] = _sigmoid(
        jnp.dot(u, wmix_ref[:, D_MODEL:2 * D_MODEL], preferred_element_type=F32)).astype(BF16)


def _inproj_vg_kernel(x_ref, g_ref, wv_ref, wg_ref, v_ref, gs_ref):
    u = _normed(x_ref, g_ref)
    v_ref[...] = jnp.dot(u, wv_ref[...], preferred_element_type=F32).astype(BF16)
    gs_ref[...] = _silu(jnp.dot(u, wg_ref[...], preferred_element_type=F32)).astype(BF16)


def _w_cols(width, start):
    assert start % width == 0
    return pl.BlockSpec((D_MODEL, width), lambda *_: (0, start // width), pipeline_mode=pl.Buffered(1))


def _in_proj_call(body, name, x2d, gain, w_bf, w_specs, extra_in, extra_specs, out_widths, tm,
                  extra_out=(), extra_out_specs=(), scratch=()):
    rows = x2d.shape[0]
    row_spec = lambda w: pl.BlockSpec((tm, w), lambda i: (i, 0))
    return pl.pallas_call(
        body,
        grid=(rows // tm,),
        in_specs=[row_spec(D_MODEL), _const_spec(gain.shape), *w_specs, *extra_specs],
        out_specs=[*[row_spec(w) for w in out_widths], *extra_out_specs],
        out_shape=[*[jax.ShapeDtypeStruct((rows, w), BF16) for w in out_widths], *extra_out],
        scratch_shapes=list(scratch),
        compiler_params=pltpu.CompilerParams(
            dimension_semantics=("arbitrary",),
            vmem_limit_bytes=VMEM_LIMIT_BYTES),
        name=name,
    )(x2d, gain, *([w_bf] * len(w_specs)), *extra_in)


def _in_proj_all(x2d, gain, w_bf, cw, cb, tail0, cos, sin, tm, tiles_per_seq, n_pad):
    c_qk = 2 * M_INNER
    c_v = c_qk + 2 * R_QK
    c_g = c_v + R_V
    c_mix = c_g + R_V
    tail_shape = (SUBLANES, M_INNER)
    xm, xc, zs, tail = _in_proj_call(
        functools.partial(_inproj_mlstm_kernel, tm=tm, tiles_per_seq=tiles_per_seq),
        "in_proj_mlstm", x2d, gain, w_bf, [_w_cols(2 * M_INNER, 0)], (cw, cb, tail0),
        (_const_spec(cw.shape), _const_spec(cb.shape), _const_spec(tail_shape)),
        (M_INNER, M_INNER, M_INNER), tm,
        extra_out=(jax.ShapeDtypeStruct(tail_shape, F32),),
        extra_out_specs=(pl.BlockSpec(tail_shape, lambda i: (0, 0)),),
        scratch=(pltpu.VMEM(tail_shape, F32), pltpu.VMEM((2, tm, CONV_STRIP), F32)))
    pos_spec = pl.BlockSpec((tm, R_QK_DIM // 2), lambda i: (i % tiles_per_seq, 0))
    qr, kr, ga, gb = _in_proj_call(
        functools.partial(_inproj_rot_gate_kernel, tm=tm, n_pad=n_pad),
        "in_proj_rot_gate", x2d, gain, w_bf, [_w_cols(2 * R_QK, c_qk), _w_cols(2 * D_MODEL, c_mix)],
        (cos, sin), (pos_spec, pos_spec), (R_QK, R_QK, D_MODEL, D_MODEL), tm)
    v, gs = _in_proj_call(_inproj_vg_kernel, "in_proj_vg", x2d, gain, w_bf,
                          [_w_cols(R_V, c_v), _w_cols(R_V, c_g)], (), (), (R_V, R_V), tm)
    return (xm, xc, zs, tail), (qr, kr, v, gs), (ga, gb)


def _head_norm(h, w):
    mu = jnp.mean(h, axis=-1, keepdims=True)
    c = h - mu
    var = jnp.mean(c * c, axis=-1, keepdims=True)
    return c * lax.rsqrt(var + EPS) * w


def _lane_cumsum(x, n):
    lane = lax.broadcasted_iota(jnp.int32, x.shape, 1)
    d = 1
    while d < n:
        x = x + jnp.where(lane >= d, pltpu.roll(x, d, axis=1), 0.0)
        d *= 2
    return x


def _mlstm_prep_kernel(xm_ref, xc_ref, bdq_ref, bdk_ref, bdv_ref, wif_ref, bif_ref,
                       q_ref, k_ref, v_ref, row_ref, col_ref, *, TP, T, n_pad):
    scale = M_HEAD_DIM ** -0.5
    for j in range(M_INNER // BD_TILE):
        sl = slice(j * BD_TILE, (j + 1) * BD_TILE)
        q_ref[:, sl] = jnp.dot(xc_ref[:, sl], bdq_ref[j], preferred_element_type=F32).astype(BF16)
        kj = jnp.dot(xc_ref[:, sl], bdk_ref[j], preferred_element_type=F32)
        k_ref[:, sl] = (kj * scale).astype(BF16)
        v_ref[:, sl] = jnp.dot(xm_ref[:, sl], bdv_ref[j], preferred_element_type=F32).astype(BF16)
    xcm = jnp.concatenate([xc_ref[...], xm_ref[...]], axis=1)
    g_all = jnp.dot(xcm, wif_ref[...], preferred_element_type=F32)[:, 0:LANES] + bif_ref[...]
    lane = lax.broadcasted_iota(jnp.int32, (T, LANES), 1)
    subl = lax.broadcasted_iota(jnp.int32, (SUBLANES, T), 0)
    for c in range(TP // T):
        g = g_all[c * T:(c + 1) * T, :]
        if n_pad:
            rowv = lax.broadcasted_iota(jnp.int32, (T, LANES), 0) >= n_pad
            g = jnp.where(jnp.logical_and(lane < M_HEADS, jnp.logical_not(rowv)), NEG, g)
        gt = g.T
        lf_t = jnp.minimum(gt, 0.0) - jnp.log1p(jnp.exp(-jnp.abs(gt)))
        cum_t = _lane_cumsum(lf_t, T)
        row_ref[c] = jnp.where(subl < M_HEADS, gt[0:SUBLANES, :], cum_t[0:SUBLANES, :])
        col_ref[c * T:(c + 1) * T, :] = jnp.where(lane < M_HEADS, g, cum_t.T)


def _mlstm_prep(xm, xc, weights, TP, T, n_pad):
    nb, L, _ = xm.shape
    blk = pl.BlockSpec((None, TP, M_INNER), lambda b, i: (b, i, 0))
    big = jax.ShapeDtypeStruct((nb, L, M_INNER), BF16)
    return pl.pallas_call(
        functools.partial(_mlstm_prep_kernel, TP=TP, T=T, n_pad=n_pad),
        grid=(nb, L // TP),
        in_specs=[blk, blk, *[_const_spec(w.shape) for w in weights]],
        out_specs=[blk, blk, blk,
                   pl.BlockSpec((None, TP // T, SUBLANES, T), lambda b, i: (b, i, 0, 0)),
                   pl.BlockSpec((None, TP, LANES), lambda b, i: (b, i, 0))],
        out_shape=[big, big, big,
                   jax.ShapeDtypeStruct((nb, L // T, SUBLANES, T), F32),
                   jax.ShapeDtypeStruct((nb, L, LANES), F32)],
        compiler_params=pltpu.CompilerParams(
            dimension_semantics=("arbitrary", "arbitrary"),
            vmem_limit_bytes=VMEM_LIMIT_BYTES),
        name="mlstm_prep",
    )(xm, xc, *weights)


def _mlstm_kernel(q_ref, k_ref, v_ref, xc_ref, zs_ref, row_ref, col_ref, skip_ref, gn_ref,
                  c0_ref, n0_ref, m0_ref, h_ref, *rest, T, nsub, emit_state):
    if emit_state:
        co_ref, no_ref, mo_ref, c_s, cb_s, n_s, m_s = rest
    else:
        c_s, cb_s, n_s, m_s = rest
    ci = pl.program_id(1)
    nc = pl.num_programs(1)

    @pl.when(ci == 0)
    def _():
        c_s[...] = c0_ref[...]
        cb_s[...] = c0_ref[...].astype(BF16)
        n_s[...] = n0_ref[...]
        m_s[...] = m0_ref[...]

    causal = (lax.broadcasted_iota(jnp.int32, (T, T), 0)
              >= lax.broadcasted_iota(jnp.int32, (T, T), 1))
    heads = range(M_HEADS)
    sls = [slice(h * M_HEAD_DIM, (h + 1) * M_HEAD_DIM) for h in heads]

    def chunk(rows, rowp):
        colp = col_ref[rows, :]

        s_raw = [lax.dot_general(q_ref[rows, sls[h]], k_ref[rows, sls[h]], _NT,
                                 preferred_element_type=F32) for h in heads]
        qc = [jnp.dot(q_ref[rows, sls[h]], cb_s[h], preferred_element_type=F32) for h in heads]

        dmat, w_inter, e_negm, w_k, decay, m_new = [], [], [], [], [], []
        for h in heads:
            i_row = rowp[h:h + 1, :]
            f_row = rowp[M_HEADS + h:M_HEADS + h + 1, :]
            i_col = colp[:, h:h + 1]
            f_col = colp[:, M_HEADS + h:M_HEADS + h + 1]
            g_tot = f_row[:, T - 1:T]
            m_prev = m_s[h][0:1, 0:1]
            b_row = i_row - f_row
            log_d = jnp.where(causal, f_col + b_row, NEG)
            inter = f_col + m_prev
            m_row = jnp.maximum(inter, jnp.max(log_d, axis=1, keepdims=True))
            dmat.append(jnp.exp(log_d - m_row))
            w_inter.append(jnp.exp(inter - m_row))
            e_negm.append(jnp.exp(-m_row))
            mn = jnp.maximum(g_tot + m_prev, jnp.max(g_tot + b_row, axis=1, keepdims=True))
            m_new.append(mn)
            w_k.append(jnp.exp(g_tot - f_col + i_col - mn))
            decay.append(jnp.exp(g_tot + m_prev - mn))

        hhs = []
        for h in heads:
            qn = jnp.sum(q_ref[rows, sls[h]].astype(F32) * n_s[h][0:1, :], axis=1, keepdims=True)
            s = s_raw[h] * dmat[h]
            num = (jnp.dot(s.astype(BF16), v_ref[rows, sls[h]], preferred_element_type=F32)
                   + w_inter[h] * qc[h])
            den = jnp.sum(s, axis=1, keepdims=True) + w_inter[h] * qn
            hhs.append(num * (1.0 / jnp.maximum(jnp.abs(den), e_negm[h])))
        for h in heads:
            kw = k_ref[rows, sls[h]].astype(F32) * w_k[h]
            c_new = decay[h] * c_s[h] + lax.dot_general(kw.astype(BF16), v_ref[rows, sls[h]], _TN,
                                                        preferred_element_type=F32)
            n_new = decay[h] * n_s[h][0:1, :] + jnp.sum(kw, axis=0, keepdims=True)
            c_s[h] = c_new
            cb_s[h] = c_new.astype(BF16)
            n_s[h] = jnp.broadcast_to(n_new, (SUBLANES, M_HEAD_DIM))
            m_s[h] = jnp.broadcast_to(m_new[h], (SUBLANES, LANES))
        for h in heads:
            hn = _head_norm(hhs[h], gn_ref[:, sls[h]])
            out = ((hn + skip_ref[:, sls[h]] * xc_ref[rows, sls[h]].astype(F32))
                   * zs_ref[rows, sls[h]].astype(F32))
            h_ref[rows, sls[h]] = out.astype(BF16)

    for cc in range(nsub):
        chunk(slice(cc * T, (cc + 1) * T), row_ref[cc])

    if emit_state:
        @pl.when(ci == nc - 1)
        def _():
            co_ref[...] = c_s[...]
            no_ref[...] = n_s[...]
            mo_ref[...] = m_s[...]


def _mlstm(q, k, v, xc, zs, rowp, colp, skip, gn, state, T, emit_state):
    nb, L, _ = q.shape
    nsub = min(MLSTM_STEP_CHUNKS, L // T)
    nc = L // (T * nsub)
    blk = pl.BlockSpec((None, T * nsub, M_INNER), lambda b, c: (b, c, 0))
    in_specs = [blk, blk, blk, blk, blk,
                pl.BlockSpec((None, nsub, SUBLANES, T), lambda b, c: (b, c, 0, 0)),
                pl.BlockSpec((None, T * nsub, LANES), lambda b, c: (b, c, 0)),
                _const_spec(skip.shape), _const_spec(gn.shape),
                *[_const_spec(s.shape) for s in state]]
    out_shape = [jax.ShapeDtypeStruct((nb, L, M_INNER), BF16)]
    out_specs = [blk]
    if emit_state:
        assert nb == 1
        for s in state:
            out_shape.append(jax.ShapeDtypeStruct(s.shape, F32))
            out_specs.append(pl.BlockSpec(s.shape, lambda b, c, _n=len(s.shape): (0,) * _n))
    c_shape = state[0].shape
    scratch = [pltpu.VMEM(c_shape, F32), pltpu.VMEM(c_shape, BF16),
               pltpu.VMEM(state[1].shape, F32), pltpu.VMEM(state[2].shape, F32)]
    return pl.pallas_call(
        functools.partial(_mlstm_kernel, T=T, nsub=nsub, emit_state=emit_state),
        grid=(nb, nc),
        in_specs=in_specs,
        out_specs=out_specs,
        out_shape=out_shape,
        scratch_shapes=scratch,
        compiler_params=pltpu.CompilerParams(
            dimension_semantics=("arbitrary", "arbitrary"),
            vmem_limit_bytes=VMEM_LIMIT_BYTES),
        name="mlstm_state" if emit_state else "mlstm",
    )(q, k, v, xc, zs, rowp, colp, skip, gn, *state)


def _retention_kernel(q_ref, k_ref, v_ref, gs_ref, gn_ref, s0_ref,
                      o_ref, *rest, T, nsub, emit_state):
    if emit_state:
        so_ref, st_s, dm_s = rest
    else:
        st_s, dm_s = rest
    ci = pl.program_id(1)
    nc = pl.num_programs(1)
    log_gamma = [math.log(1.0 - 2.0 ** (-5.0 - h)) for h in range(R_HEADS)]

    @pl.when(ci == 0)
    def _():
        st_s[...] = s0_ref[...]
        diff = (lax.broadcasted_iota(jnp.int32, (T, T), 0)
                - lax.broadcasted_iota(jnp.int32, (T, T), 1))
        dfl = diff.astype(F32)
        for h in range(R_HEADS):
            dm_s[h] = jnp.where(diff >= 0, jnp.exp(log_gamma[h] * dfl), 0.0)

    tf = lax.broadcasted_iota(jnp.int32, (T, 1), 0).astype(F32)
    for cc in range(nsub):
        rows = slice(cc * T, (cc + 1) * T)
        for h in range(R_HEADS):
            qh = q_ref[rows, h * R_QK_DIM:(h + 1) * R_QK_DIM]
            kh = k_ref[rows, h * R_QK_DIM:(h + 1) * R_QK_DIM]
            sl = slice(h * R_V_DIM, (h + 1) * R_V_DIM)
            vh = v_ref[rows, sl]
            s = lax.dot_general(qh, kh, _NT, preferred_element_type=F32) * dm_s[h]
            q_decay = jnp.exp(log_gamma[h] * (tf + 1.0))
            k_decay = jnp.exp(log_gamma[h] * (T - 1.0 - tf))
            o = (jnp.dot(s.astype(BF16), vh, preferred_element_type=F32)
                 + q_decay * jnp.dot(qh, st_s[h].astype(BF16), preferred_element_type=F32))
            kd = (kh.astype(F32) * k_decay).astype(BF16)
            st_s[h] = (math.exp(log_gamma[h] * T) * st_s[h]
                       + lax.dot_general(kd, vh, _TN, preferred_element_type=F32))
            o_ref[rows, sl] = (_head_norm(o, gn_ref[:, sl])
                               * gs_ref[rows, sl].astype(F32)).astype(BF16)

    if emit_state:
        @pl.when(ci == nc - 1)
        def _():
            so_ref[...] = st_s[...]


def _retention(q, k, v, gs, gn, s0, T, emit_state):
    nb, L, _ = q.shape
    nsub = min(RET_STEP_CHUNKS, L // T)
    nc = L // (T * nsub)
    qk_blk = pl.BlockSpec((None, T * nsub, R_QK), lambda b, c: (b, c, 0))
    v_blk = pl.BlockSpec((None, T * nsub, R_V), lambda b, c: (b, c, 0))
    in_specs = [qk_blk, qk_blk, v_blk, v_blk, _const_spec(gn.shape), _const_spec(s0.shape)]
    out_shape = [jax.ShapeDtypeStruct((nb, L, R_V), BF16)]
    out_specs = [v_blk]
    if emit_state:
        assert nb == 1
        out_shape.append(jax.ShapeDtypeStruct(s0.shape, F32))
        out_specs.append(pl.BlockSpec(s0.shape, lambda b, c: (0, 0, 0)))
    scratch = [pltpu.VMEM(s0.shape, F32), pltpu.VMEM((R_HEADS, T, T), F32)]
    return pl.pallas_call(
        functools.partial(_retention_kernel, T=T, nsub=nsub, emit_state=emit_state),
        grid=(nb, nc),
        in_specs=in_specs,
        out_specs=out_specs,
        out_shape=out_shape,
        scratch_shapes=scratch,
        compiler_params=pltpu.CompilerParams(
            dimension_semantics=("arbitrary", "arbitrary"),
            vmem_limit_bytes=VMEM_LIMIT_BYTES),
        name="retention_state" if emit_state else "retention",
    )(q, k, v, gs, gn, s0)


def _rms(x, g):
    return x * lax.rsqrt(jnp.mean(x * x, axis=-1, keepdims=True) + EPS) * g


def _out_ffn_kernel(hm_ref, hr_ref, ga_ref, gb_ref, x_ref, wpm_ref, wpr_ref, wout_ref,
                    wup_ref, wdn_ref, g1_ref, g2_ref, g3_ref, o_ref):
    ya = jnp.dot(hm_ref[...], wpm_ref[...], preferred_element_type=F32)
    yb = jnp.dot(hr_ref[...], wpr_ref[...], preferred_element_type=F32)
    mixin = ga_ref[...].astype(F32) * ya + gb_ref[...].astype(F32) * yb
    mix = jnp.dot(mixin.astype(BF16), wout_ref[...], preferred_element_type=F32)
    h1 = x_ref[...] + _rms(mix, g1_ref[...])
    u = _rms(h1, g2_ref[...]).astype(BF16)
    f = None
    for j in range(D_FF // FF_TILE):
        sl = slice(j * FF_TILE, (j + 1) * FF_TILE)
        a = jnp.maximum(jnp.dot(u, wup_ref[:, sl], preferred_element_type=F32), 0.0)
        part = jnp.dot((a * a).astype(BF16), wdn_ref[sl, :], preferred_element_type=F32)
        f = part if f is None else f + part
    o_ref[...] = h1 + _rms(f, g3_ref[...])


def _out_ffn(hm, hr, ga, gb, x2d, wpm, wpr, wout, wup, wdn, g1, g2, g3, tm):
    rows = x2d.shape[0]
    row_spec = lambda w: pl.BlockSpec((tm, w), lambda i: (i, 0))
    consts = (wpm, wpr, wout, wup, wdn, g1, g2, g3)
    return pl.pallas_call(
        _out_ffn_kernel,
        grid=(rows // tm,),
        in_specs=[row_spec(M_INNER), row_spec(R_V), row_spec(D_MODEL), row_spec(D_MODEL),
                  row_spec(D_MODEL), *[_const_spec(c.shape) for c in consts]],
        out_specs=row_spec(D_MODEL),
        out_shape=jax.ShapeDtypeStruct((rows, D_MODEL), F32),
        compiler_params=pltpu.CompilerParams(
            dimension_semantics=("arbitrary",),
            vmem_limit_bytes=VMEM_LIMIT_BYTES),
        name="out_ffn",
    )(hm, hr, ga, gb, x2d, *consts)


def _block_diag_tiles(w):
    rows = w.reshape(M_INNER, M_QKV_BLOCK)
    wide = jnp.tile(rows, (1, BD_TILE // M_QKV_BLOCK))
    r_blk = (lax.broadcasted_iota(jnp.int32, wide.shape, 0) % BD_TILE) // M_QKV_BLOCK
    c_blk = lax.broadcasted_iota(jnp.int32, wide.shape, 1) // M_QKV_BLOCK
    t = jnp.where(r_blk == c_blk, wide, 0.0).astype(BF16)
    return t.reshape(M_INNER // BD_TILE, BD_TILE, BD_TILE)


def _rope_tables(pos0, n):
    half = R_QK_DIM // 2
    inv = ROPE_BASE ** (-np.arange(half, dtype=np.float64) / half)
    ang = (pos0 + np.arange(n, dtype=np.float64))[:, None] * inv[None, :]
    return jnp.asarray(np.cos(ang), F32), jnp.asarray(np.sin(ang), F32)


def kernel(x, meta_tokens, norm_mix_pre, w_in, conv_w, conv_b, w_q_m, w_k_m, w_v_m, w_if, b_if,
           skip_m, gn_m, gn_r, w_proj_m, w_proj_r, w_out, norm_mix_post, norm_ffn_pre, w_up,
           w_down, norm_ffn_post):
    nb, seq, _ = x.shape
    assert norm_mix_pre.shape[0] == 1, "single-layer block"
    assert seq % PROJ_TM == 0 and seq % PREP_TP == 0 and seq % (RET_STEP_CHUNKS * CHUNK_T) == 0
    row = lambda a: a[0].reshape(1, -1).astype(F32)
    n_pad = META_T - N_META

    w_in_bf = w_in[0].astype(BF16)
    g_pre = row(norm_mix_pre)
    cw, cb = conv_w[0].astype(F32), row(conv_b)

    meta_rows = jnp.concatenate(
        [jnp.zeros((n_pad, D_MODEL), x.dtype), meta_tokens.astype(x.dtype)], axis=0)
    cos_m, sin_m = _rope_tables(-n_pad, META_T)
    cos_x, sin_x = _rope_tables(N_META, seq)
    tail_zero = jnp.zeros((SUBLANES, M_INNER), F32)
    m_meta, r_meta, _ = _in_proj_all(meta_rows, g_pre, w_in_bf, cw, cb, tail_zero, cos_m, sin_m,
                                     META_T, 1, n_pad)
    x2d = x.reshape(nb * seq, D_MODEL)
    m_in, r_in, (ga, gb) = _in_proj_all(x2d, g_pre, w_in_bf, cw, cb, m_meta[3], cos_x, sin_x,
                                        PROJ_TM, seq // PROJ_TM, 0)

    nblk = M_INNER // M_QKV_BLOCK
    wif3 = w_if[0].reshape(3, nblk, M_QKV_BLOCK, 2 * M_HEADS)
    fold = lambda w, part: jnp.einsum('nio,noj->nij', w, part).reshape(M_INNER, 2 * M_HEADS)
    wg = jnp.concatenate([fold(w_q_m[0], wif3[0]) + fold(w_k_m[0], wif3[1]), fold(w_v_m[0], wif3[2])],
                         axis=0)
    wif = jnp.zeros((2 * M_INNER, GATE_PAD), F32).at[:, :2 * M_HEADS].set(wg).astype(BF16)
    bif = jnp.zeros((1, LANES), F32).at[0, :2 * M_HEADS].set(b_if[0])
    p_weights = (_block_diag_tiles(w_q_m[0]), _block_diag_tiles(w_k_m[0]),
                 _block_diag_tiles(w_v_m[0]), wif, bif)
    m_zero = (jnp.zeros((M_HEADS, M_HEAD_DIM, M_HEAD_DIM), F32),
              jnp.zeros((M_HEADS, SUBLANES, M_HEAD_DIM), F32),
              jnp.zeros((M_HEADS, SUBLANES, LANES), F32))
    as_seq = lambda a, n, t: a.reshape(n, t, a.shape[-1])
    xm_m, xc_m, zs_m = [as_seq(a, 1, META_T) for a in m_meta[:3]]
    xm_x, xc_x, zs_x = [as_seq(a, nb, seq) for a in m_in[:3]]
    prep_m = _mlstm_prep(xm_m, xc_m, p_weights, META_T, META_T, n_pad)
    m_init = _mlstm(*prep_m[:3], xc_m, zs_m, *prep_m[3:], row(skip_m), row(gn_m), m_zero,
                    META_T, True)[1:]
    prep_x = _mlstm_prep(xm_x, xc_x, p_weights, PREP_TP, CHUNK_T, 0)
    hm = _mlstm(*prep_x[:3], xc_x, zs_x, *prep_x[3:], row(skip_m), row(gn_m), tuple(m_init),
                CHUNK_T, False)[0]

    r_zero = jnp.zeros((R_HEADS, R_QK_DIM, R_V_DIM), F32)
    r_init = _retention(*[as_seq(a, 1, META_T) for a in r_meta], row(gn_r), r_zero,
                        META_T, True)[1]
    hr = _retention(*[as_seq(a, nb, seq) for a in r_in], row(gn_r), r_init, CHUNK_T, False)[0]

    out = _out_ffn(hm.reshape(nb * seq, M_INNER), hr.reshape(nb * seq, R_V), ga, gb, x2d,
                   w_proj_m[0].astype(BF16), w_proj_r[0].astype(BF16), w_out[0].astype(BF16),
                   w_up[0].astype(BF16), w_down[0].astype(BF16),
                   row(norm_mix_post), row(norm_ffn_pre), row(norm_ffn_post), OUT_TM)
    return out.reshape(nb, seq, D_MODEL)
```

```python
import functools
import math

import jax
import jax.numpy as jnp
import numpy as np
from jax import lax
from jax.experimental import pallas as pl
from jax.experimental.pallas import tpu as pltpu

F32 = jnp.float32
BF16 = jnp.bfloat16

D_MODEL = 1024
N_META = 16
M_INNER = 2 * D_MODEL
M_HEADS = 4
M_HEAD_DIM = M_INNER // M_HEADS
M_QKV_BLOCK = 4
M_CONV = 4
R_HEADS = 4
R_QK_DIM = D_MODEL // R_HEADS
R_V_DIM = 2 * R_QK_DIM
R_QK = R_HEADS * R_QK_DIM
R_V = R_HEADS * R_V_DIM
D_FF = 4 * D_MODEL
ROPE_BASE = 10000.0
EPS = 1e-6
NEG = -1e30
LOG2E = math.log2(math.e)
N_IN = 2 * M_INNER + 2 * R_QK + 2 * R_V + 2 * D_MODEL

LANES = 128
SUBLANES = 8
MXU_DIM = 256
VMEM_LIMIT_BYTES = 56 * 1024 * 1024

CHUNK_T = 256
META_T = 128
BD_TILE = MXU_DIM
GATE_PAD = MXU_DIM
MLSTM_STEP_CHUNKS = 1
RET_STEP_CHUNKS = 4
PREP_TP = 1024
CONV_STRIP = 256
PROJ_TM = 1024
PROJ_TM_CONV = 512
OUT_TM = 512
FF_TILE = 1024

_NT = (((1,), (1,)), ((), ()))
_TN = (((0,), (0,)), ((), ()))


def _const_spec(shape):
    nd = len(shape)
    return pl.BlockSpec(shape, lambda *_: (0,) * nd, pipeline_mode=pl.Buffered(1))


def _sigmoid(x):
    return 1.0 / (1.0 + jnp.exp2(x * (-LOG2E)))


def _silu(x):
    return x * _sigmoid(x)


def _normed(x_ref, g_ref):
    x = x_ref[...]
    ms = jnp.mean(x * x, axis=-1, keepdims=True)
    return (x * lax.rsqrt(ms + EPS) * g_ref[...]).astype(BF16)


def _inproj_mlstm_kernel(x_ref, g_ref, w_ref, cw_ref, cb_ref, t0_ref,
                         xm_ref, xc_ref, zs_ref, to_ref, halo_s, res_s, *, tm, tiles_per_seq):
    i = pl.program_id(0)

    @pl.when(i % tiles_per_seq == 0)
    def _():
        halo_s[...] = t0_ref[...]

    u = _normed(x_ref, g_ref)
    nstrip = M_INNER // CONV_STRIP
    nv = tm // SUBLANES
    sub = lax.broadcasted_iota(jnp.int32, (nv, SUBLANES, CONV_STRIP), 1)

    def strip_dot(c):
        sl = slice(c * CONV_STRIP, (c + 1) * CONV_STRIP)
        res_s[(i + c) % 2] = jnp.dot(u, w_ref[:, sl], preferred_element_type=F32)

    strip_dot(0)
    for c in range(nstrip):
        sl = slice(c * CONV_STRIP, (c + 1) * CONV_STRIP)
        if c + 1 < nstrip:
            strip_dot(c + 1)
        zsl = slice(M_INNER + c * CONV_STRIP, M_INNER + (c + 1) * CONV_STRIP)
        zm = jnp.dot(u, w_ref[:, zsl], preferred_element_type=F32)
        zs_ref[:, sl] = _silu(zm).astype(BF16)
        xm = res_s[(i + c) % 2]
        xm_ref[:, sl] = xm.astype(BF16)
        x3 = xm.reshape(nv, SUBLANES, CONV_STRIP)
        prev3 = jnp.concatenate([halo_s[:, sl].reshape(1, SUBLANES, CONV_STRIP), x3[:nv - 1]], axis=0)
        y = cb_ref[:, sl] + cw_ref[M_CONV - 1:M_CONV, sl] * x3
        for j in range(M_CONV - 1):
            sh = M_CONV - 1 - j
            merged = jnp.where(sub >= SUBLANES - sh, prev3, x3)
            y = y + cw_ref[j:j + 1, sl] * pltpu.roll(merged, sh, axis=1)
        halo_s[:, sl] = xm[tm - SUBLANES:, :]
        xc_ref[:, sl] = _silu(y.reshape(tm, CONV_STRIP)).astype(BF16)

    @pl.when(i == pl.num_programs(0) - 1)
    def _():
        to_ref[...] = halo_s[...]


def _inproj_rot_gate_kernel(x_ref, g_ref, w_ref, wmix_ref, cos_ref, sin_ref,
                            q_ref, k_ref, ga_ref, gb_ref, *, tm, n_pad):
    u = _normed(x_ref, g_ref)
    cos = cos_ref[...]
    sin = sin_ref[...]
    half = R_QK_DIM // 2
    kscale = R_QK_DIM ** -0.5
    if n_pad:
        kmul = (lax.broadcasted_iota(jnp.int32, (tm, 1), 0) >= n_pad).astype(F32) * kscale

    def rot(t, h):
        t1 = t[:, h * R_QK_DIM:h * R_QK_DIM + half]
        t2 = t[:, h * R_QK_DIM + half:(h + 1) * R_QK_DIM]
        return t1 * cos - t2 * sin, t1 * sin + t2 * cos

    q = jnp.dot(u, w_ref[:, 0:R_QK], preferred_element_type=F32)
    for h in range(R_HEADS):
        a, b = rot(q, h)
        q_ref[:, h * R_QK_DIM:h * R_QK_DIM + half] = a.astype(BF16)
        q_ref[:, h * R_QK_DIM + half:(h + 1) * R_QK_DIM] = b.astype(BF16)
    k = jnp.dot(u, w_ref[:, R_QK:2 * R_QK], preferred_element_type=F32)
    for h in range(R_HEADS):
        a, b = rot(k, h)
        if n_pad:
            a, b = a * kmul, b * kmul
        else:
            a, b = a * kscale, b * kscale
        k_ref[:, h * R_QK_DIM:h * R_QK_DIM + half] = a.astype(BF16)
        k_ref[:, h * R_QK_DIM + half:(h + 1) * R_QK_DIM] = b.astype(BF16)
    ga_ref[...] = _sigmoid(jnp.dot(u, wmix_ref[:, 0:D_MODEL], preferred_element_type=F32)).astype(BF16)
    gb_ref[...] = _sigmoid(
        jnp.dot(u, wmix_ref[:, D_MODEL:2 * D_MODEL], preferred_element_type=F32)).astype(BF16)


def _inproj_vg_kernel(x_ref, g_ref, wv_ref, wg_ref, v_ref, gs_ref):
    u = _normed(x_ref, g_ref)
    v_ref[...] = jnp.dot(u, wv_ref[...], preferred_element_type=F32).astype(BF16)
    gs_ref[...] = _silu(jnp.dot(u, wg_ref[...], preferred_element_type=F32)).astype(BF16)


def _w_cols(width, start):
    assert start % width == 0
    return pl.BlockSpec((D_MODEL, width), lambda *_: (0, start // width), pipeline_mode=pl.Buffered(1))


def _in_proj_call(body, name, x2d, gain, w_bf, w_specs, extra_in, extra_specs, out_widths, tm,
                  extra_out=(), extra_out_specs=(), scratch=()):
    rows = x2d.shape[0]
    row_spec = lambda w: pl.BlockSpec((tm, w), lambda i: (i, 0))
    return pl.pallas_call(
        body,
        grid=(rows // tm,),
        in_specs=[row_spec(D_MODEL), _const_spec(gain.shape), *w_specs, *extra_specs],
        out_specs=[*[row_spec(w) for w in out_widths], *extra_out_specs],
        out_shape=[*[jax.ShapeDtypeStruct((rows, w), BF16) for w in out_widths], *extra_out],
        scratch_shapes=list(scratch),
        compiler_params=pltpu.CompilerParams(
            dimension_semantics=("arbitrary",),
            vmem_limit_bytes=VMEM_LIMIT_BYTES),
        name=name,
    )(x2d, gain, *([w_bf] * len(w_specs)), *extra_in)


def _in_proj_all(x2d, gain, w_bf, cw, cb, tail0, cos, sin, tm, tiles_per_seq, n_pad):
    c_qk = 2 * M_INNER
    c_v = c_qk + 2 * R_QK
    c_g = c_v + R_V
    c_mix = c_g + R_V
    tail_shape = (SUBLANES, M_INNER)
    tmc = min(tm, PROJ_TM_CONV)
    xm, xc, zs, tail = _in_proj_call(
        functools.partial(_inproj_mlstm_kernel, tm=tmc, tiles_per_seq=tiles_per_seq * tm // tmc),
        "in_proj_mlstm", x2d, gain, w_bf, [_w_cols(2 * M_INNER, 0)], (cw, cb, tail0),
        (_const_spec(cw.shape), _const_spec(cb.shape), _const_spec(tail_shape)),
        (M_INNER, M_INNER, M_INNER), tmc,
        extra_out=(jax.ShapeDtypeStruct(tail_shape, F32),),
        extra_out_specs=(pl.BlockSpec(tail_shape, lambda i: (0, 0)),),
        scratch=(pltpu.VMEM(tail_shape, F32), pltpu.VMEM((2, tmc, CONV_STRIP), F32)))
    pos_spec = pl.BlockSpec((tm, R_QK_DIM // 2), lambda i: (i % tiles_per_seq, 0))
    qr, kr, ga, gb = _in_proj_call(
        functools.partial(_inproj_rot_gate_kernel, tm=tm, n_pad=n_pad),
        "in_proj_rot_gate", x2d, gain, w_bf, [_w_cols(2 * R_QK, c_qk), _w_cols(2 * D_MODEL, c_mix)],
        (cos, sin), (pos_spec, pos_spec), (R_QK, R_QK, D_MODEL, D_MODEL), tm)
    v, gs = _in_proj_call(_inproj_vg_kernel, "in_proj_vg", x2d, gain, w_bf,
                          [_w_cols(R_V, c_v), _w_cols(R_V, c_g)], (), (), (R_V, R_V), tm)
    return (xm, xc, zs, tail), (qr, kr, v, gs), (ga, gb)


def _head_norm(h, w):
    mu = jnp.mean(h, axis=-1, keepdims=True)
    c = h - mu
    var = jnp.mean(c * c, axis=-1, keepdims=True)
    return c * lax.rsqrt(var + EPS) * w


def _lane_cumsum(x, n):
    lane = lax.broadcasted_iota(jnp.int32, x.shape, 1)
    d = 1
    while d < n:
        x = x + jnp.where(lane >= d, pltpu.roll(x, d, axis=1), 0.0)
        d *= 2
    return x


def _mlstm_prep_kernel(xm_ref, xc_ref, bdq_ref, bdk_ref, bdv_ref, wif_ref, bif_ref,
                       q_ref, k_ref, v_ref, row_ref, col_ref, *, TP, T, n_pad):
    scale = M_HEAD_DIM ** -0.5
    for j in range(M_INNER // BD_TILE):
        sl = slice(j * BD_TILE, (j + 1) * BD_TILE)
        q_ref[:, sl] = jnp.dot(xc_ref[:, sl], bdq_ref[j], preferred_element_type=F32).astype(BF16)
        kj = jnp.dot(xc_ref[:, sl], bdk_ref[j], preferred_element_type=F32)
        k_ref[:, sl] = (kj * scale).astype(BF16)
        v_ref[:, sl] = jnp.dot(xm_ref[:, sl], bdv_ref[j], preferred_element_type=F32).astype(BF16)
    xcm = jnp.concatenate([xc_ref[...], xm_ref[...]], axis=1)
    g_all = jnp.dot(xcm, wif_ref[...], preferred_element_type=F32)[:, 0:LANES] + bif_ref[...]
    lane = lax.broadcasted_iota(jnp.int32, (T, LANES), 1)
    subl = lax.broadcasted_iota(jnp.int32, (SUBLANES, T), 0)
    for c in range(TP // T):
        g = g_all[c * T:(c + 1) * T, :]
        if n_pad:
            rowv = lax.broadcasted_iota(jnp.int32, (T, LANES), 0) >= n_pad
            g = jnp.where(jnp.logical_and(lane < M_HEADS, jnp.logical_not(rowv)), NEG, g)
        gt = g.T
        lf_t = jnp.minimum(gt, 0.0) - jnp.log1p(jnp.exp(-jnp.abs(gt)))
        cum_t = _lane_cumsum(lf_t, T)
        row_ref[c] = jnp.where(subl < M_HEADS, gt[0:SUBLANES, :], cum_t[0:SUBLANES, :])
        col_ref[c * T:(c + 1) * T, :] = jnp.where(lane < M_HEADS, g, cum_t.T)


def _mlstm_prep(xm, xc, weights, TP, T, n_pad):
    nb, L, _ = xm.shape
    blk = pl.BlockSpec((None, TP, M_INNER), lambda b, i: (b, i, 0))
    big = jax.ShapeDtypeStruct((nb, L, M_INNER), BF16)
    return pl.pallas_call(
        functools.partial(_mlstm_prep_kernel, TP=TP, T=T, n_pad=n_pad),
        grid=(nb, L // TP),
        in_specs=[blk, blk, *[_const_spec(w.shape) for w in weights]],
        out_specs=[blk, blk, blk,
                   pl.BlockSpec((None, TP // T, SUBLANES, T), lambda b, i: (b, i, 0, 0)),
                   pl.BlockSpec((None, TP, LANES), lambda b, i: (b, i, 0))],
        out_shape=[big, big, big,
                   jax.ShapeDtypeStruct((nb, L // T, SUBLANES, T), F32),
                   jax.ShapeDtypeStruct((nb, L, LANES), F32)],
        compiler_params=pltpu.CompilerParams(
            dimension_semantics=("arbitrary", "arbitrary"),
            vmem_limit_bytes=VMEM_LIMIT_BYTES),
        name="mlstm_prep",
    )(xm, xc, *weights)


def _mlstm_kernel(q_ref, k_ref, v_ref, xc_ref, zs_ref, row_ref, col_ref, skip_ref, gn_ref,
                  c0_ref, n0_ref, m0_ref, h_ref, *rest, T, nsub, emit_state):
    if emit_state:
        co_ref, no_ref, mo_ref, c_s, cb_s, n_s, m_s = rest
    else:
        c_s, cb_s, n_s, m_s = rest
    ci = pl.program_id(1)
    nc = pl.num_programs(1)

    @pl.when(ci == 0)
    def _():
        c_s[...] = c0_ref[...]
        cb_s[...] = c0_ref[...].astype(BF16)
        n_s[...] = n0_ref[...]
        m_s[...] = m0_ref[...]

    causal = (lax.broadcasted_iota(jnp.int32, (T, T), 0)
              >= lax.broadcasted_iota(jnp.int32, (T, T), 1))
    heads = range(M_HEADS)
    sls = [slice(h * M_HEAD_DIM, (h + 1) * M_HEAD_DIM) for h in heads]

    def chunk(rows, rowp):
        colp = col_ref[rows, :]

        s_raw = [lax.dot_general(q_ref[rows, sls[h]], k_ref[rows, sls[h]], _NT,
                                 preferred_element_type=F32) for h in heads]
        qc = [jnp.dot(q_ref[rows, sls[h]], cb_s[h], preferred_element_type=F32) for h in heads]

        dmat, w_inter, e_negm, w_k, decay, m_new = [], [], [], [], [], []
        for h in heads:
            i_row = rowp[h:h + 1, :]
            f_row = rowp[M_HEADS + h:M_HEADS + h + 1, :]
            i_col = colp[:, h:h + 1]
            f_col = colp[:, M_HEADS + h:M_HEADS + h + 1]
            g_tot = f_row[:, T - 1:T]
            m_prev = m_s[h][0:1, 0:1]
            b_row = i_row - f_row
            log_d = jnp.where(causal, f_col + b_row, NEG)
            inter = f_col + m_prev
            m_row = jnp.maximum(inter, jnp.max(log_d, axis=1, keepdims=True))
            dmat.append(jnp.exp(log_d - m_row))
            w_inter.append(jnp.exp(inter - m_row))
            e_negm.append(jnp.exp(-m_row))
            mn = jnp.maximum(g_tot + m_prev, jnp.max(g_tot + b_row, axis=1, keepdims=True))
            m_new.append(mn)
            w_k.append(jnp.exp(g_tot - f_col + i_col - mn))
            decay.append(jnp.exp(g_tot + m_prev - mn))

        hhs = []
        for h in heads:
            qn = jnp.sum(q_ref[rows, sls[h]].astype(F32) * n_s[h][0:1, :], axis=1, keepdims=True)
            s = s_raw[h] * dmat[h]
            num = (jnp.dot(s.astype(BF16), v_ref[rows, sls[h]], preferred_element_type=F32)
                   + w_inter[h] * qc[h])
            den = jnp.sum(s, axis=1, keepdims=True) + w_inter[h] * qn
            hhs.append(num * (1.0 / jnp.maximum(jnp.abs(den), e_negm[h])))
        for h in heads:
            kw = k_ref[rows, sls[h]].astype(F32) * w_k[h]
            c_new = decay[h] * c_s[h] + lax.dot_general(kw.astype(BF16), v_ref[rows, sls[h]], _TN,
                                                        preferred_element_type=F32)
            n_new = decay[h] * n_s[h][0:1, :] + jnp.sum(kw, axis=0, keepdims=True)
            c_s[h] = c_new
            cb_s[h] = c_new.astype(BF16)
            n_s[h] = jnp.broadcast_to(n_new, (SUBLANES, M_HEAD_DIM))
            m_s[h] = jnp.broadcast_to(m_new[h], (SUBLANES, LANES))
        for h in heads:
            hn = _head_norm(hhs[h], gn_ref[:, sls[h]])
            out = ((hn + skip_ref[:, sls[h]] * xc_ref[rows, sls[h]].astype(F32))
                   * zs_ref[rows, sls[h]].astype(F32))
            h_ref[rows, sls[h]] = out.astype(BF16)

    for cc in range(nsub):
        chunk(slice(cc * T, (cc + 1) * T), row_ref[cc])

    if emit_state:
        @pl.when(ci == nc - 1)
        def _():
            co_ref[...] = c_s[...]
            no_ref[...] = n_s[...]
            mo_ref[...] = m_s[...]


def _mlstm(q, k, v, xc, zs, rowp, colp, skip, gn, state, T, emit_state):
    nb, L, _ = q.shape
    nsub = min(MLSTM_STEP_CHUNKS, L // T)
    nc = L // (T * nsub)
    blk = pl.BlockSpec((None, T * nsub, M_INNER), lambda b, c: (b, c, 0))
    in_specs = [blk, blk, blk, blk, blk,
                pl.BlockSpec((None, nsub, SUBLANES, T), lambda b, c: (b, c, 0, 0)),
                pl.BlockSpec((None, T * nsub, LANES), lambda b, c: (b, c, 0)),
                _const_spec(skip.shape), _const_spec(gn.shape),
                *[_const_spec(s.shape) for s in state]]
    out_shape = [jax.ShapeDtypeStruct((nb, L, M_INNER), BF16)]
    out_specs = [blk]
    if emit_state:
        assert nb == 1
        for s in state:
            out_shape.append(jax.ShapeDtypeStruct(s.shape, F32))
            out_specs.append(pl.BlockSpec(s.shape, lambda b, c, _n=len(s.shape): (0,) * _n))
    c_shape = state[0].shape
    scratch = [pltpu.VMEM(c_shape, F32), pltpu.VMEM(c_shape, BF16),
               pltpu.VMEM(state[1].shape, F32), pltpu.VMEM(state[2].shape, F32)]
    return pl.pallas_call(
        functools.partial(_mlstm_kernel, T=T, nsub=nsub, emit_state=emit_state),
        grid=(nb, nc),
        in_specs=in_specs,
        out_specs=out_specs,
        out_shape=out_shape,
        scratch_shapes=scratch,
        compiler_params=pltpu.CompilerParams(
            dimension_semantics=("arbitrary", "arbitrary"),
            vmem_limit_bytes=VMEM_LIMIT_BYTES),
        name="mlstm_state" if emit_state else "mlstm",
    )(q, k, v, xc, zs, rowp, colp, skip, gn, *state)


def _retention_kernel(q_ref, k_ref, v_ref, gs_ref, gn_ref, s0_ref,
                      o_ref, *rest, T, nsub, emit_state):
    if emit_state:
        so_ref, st_s, dm_s = rest
    else:
        st_s, dm_s = rest
    ci = pl.program_id(1)
    nc = pl.num_programs(1)
    log_gamma = [math.log(1.0 - 2.0 ** (-5.0 - h)) for h in range(R_HEADS)]

    @pl.when(ci == 0)
    def _():
        st_s[...] = s0_ref[...]
        diff = (lax.broadcasted_iota(jnp.int32, (T, T), 0)
                - lax.broadcasted_iota(jnp.int32, (T, T), 1))
        dfl = diff.astype(F32)
        for h in range(R_HEADS):
            dm_s[h] = jnp.where(diff >= 0, jnp.exp(log_gamma[h] * dfl), 0.0)

    tf = lax.broadcasted_iota(jnp.int32, (T, 1), 0).astype(F32)
    for cc in range(nsub):
        rows = slice(cc * T, (cc + 1) * T)
        for h in range(R_HEADS):
            qh = q_ref[rows, h * R_QK_DIM:(h + 1) * R_QK_DIM]
            kh = k_ref[rows, h * R_QK_DIM:(h + 1) * R_QK_DIM]
            sl = slice(h * R_V_DIM, (h + 1) * R_V_DIM)
            vh = v_ref[rows, sl]
            s = lax.dot_general(qh, kh, _NT, preferred_element_type=F32) * dm_s[h]
            q_decay = jnp.exp(log_gamma[h] * (tf + 1.0))
            k_decay = jnp.exp(log_gamma[h] * (T - 1.0 - tf))
            o = (jnp.dot(s.astype(BF16), vh, preferred_element_type=F32)
                 + q_decay * jnp.dot(qh, st_s[h].astype(BF16), preferred_element_type=F32))
            kd = (kh.astype(F32) * k_decay).astype(BF16)
            st_s[h] = (math.exp(log_gamma[h] * T) * st_s[h]
                       + lax.dot_general(kd, vh, _TN, preferred_element_type=F32))
            o_ref[rows, sl] = (_head_norm(o, gn_ref[:, sl])
                               * gs_ref[rows, sl].astype(F32)).astype(BF16)

    if emit_state:
        @pl.when(ci == nc - 1)
        def _():
            so_ref[...] = st_s[...]


def _retention(q, k, v, gs, gn, s0, T, emit_state):
    nb, L, _ = q.shape
    nsub = min(RET_STEP_CHUNKS, L // T)
    nc = L // (T * nsub)
    qk_blk = pl.BlockSpec((None, T * nsub, R_QK), lambda b, c: (b, c, 0))
    v_blk = pl.BlockSpec((None, T * nsub, R_V), lambda b, c: (b, c, 0))
    in_specs = [qk_blk, qk_blk, v_blk, v_blk, _const_spec(gn.shape), _const_spec(s0.shape)]
    out_shape = [jax.ShapeDtypeStruct((nb, L, R_V), BF16)]
    out_specs = [v_blk]
    if emit_state:
        assert nb == 1
        out_shape.append(jax.ShapeDtypeStruct(s0.shape, F32))
        out_specs.append(pl.BlockSpec(s0.shape, lambda b, c: (0, 0, 0)))
    scratch = [pltpu.VMEM(s0.shape, F32), pltpu.VMEM((R_HEADS, T, T), F32)]
    return pl.pallas_call(
        functools.partial(_retention_kernel, T=T, nsub=nsub, emit_state=emit_state),
        grid=(nb, nc),
        in_specs=in_specs,
        out_specs=out_specs,
        out_shape=out_shape,
        scratch_shapes=scratch,
        compiler_params=pltpu.CompilerParams(
            dimension_semantics=("arbitrary", "arbitrary"),
            vmem_limit_bytes=VMEM_LIMIT_BYTES),
        name="retention_state" if emit_state else "retention",
    )(q, k, v, gs, gn, s0)


def _rms(x, g):
    return x * lax.rsqrt(jnp.mean(x * x, axis=-1, keepdims=True) + EPS) * g


def _out_ffn_kernel(hm_ref, hr_ref, ga_ref, gb_ref, x_ref, wpm_ref, wpr_ref, wout_ref,
                    wup_ref, wdn_ref, g1_ref, g2_ref, g3_ref, o_ref):
    ya = jnp.dot(hm_ref[...], wpm_ref[...], preferred_element_type=F32)
    yb = jnp.dot(hr_ref[...], wpr_ref[...], preferred_element_type=F32)
    mixin = ga_ref[...].astype(F32) * ya + gb_ref[...].astype(F32) * yb
    mix = jnp.dot(mixin.astype(BF16), wout_ref[...], preferred_element_type=F32)
    h1 = x_ref[...] + _rms(mix, g1_ref[...])
    u = _rms(h1, g2_ref[...]).astype(BF16)
    f = None
    for j in range(D_FF // FF_TILE):
        sl = slice(j * FF_TILE, (j + 1) * FF_TILE)
        a = jnp.maximum(jnp.dot(u, wup_ref[:, sl], preferred_element_type=F32), 0.0)
        part = jnp.dot((a * a).astype(BF16), wdn_ref[sl, :], preferred_element_type=F32)
        f = part if f is None else f + part
    o_ref[...] = h1 + _rms(f, g3_ref[...])


def _out_ffn(hm, hr, ga, gb, x2d, wpm, wpr, wout, wup, wdn, g1, g2, g3, tm):
    rows = x2d.shape[0]
    row_spec = lambda w: pl.BlockSpec((tm, w), lambda i: (i, 0))
    consts = (wpm, wpr, wout, wup, wdn, g1, g2, g3)
    return pl.pallas_call(
        _out_ffn_kernel,
        grid=(rows // tm,),
        in_specs=[row_spec(M_INNER), row_spec(R_V), row_spec(D_MODEL), row_spec(D_MODEL),
                  row_spec(D_MODEL), *[_const_spec(c.shape) for c in consts]],
        out_specs=row_spec(D_MODEL),
        out_shape=jax.ShapeDtypeStruct((rows, D_MODEL), F32),
        compiler_params=pltpu.CompilerParams(
            dimension_semantics=("arbitrary",),
            vmem_limit_bytes=VMEM_LIMIT_BYTES),
        name="out_ffn",
    )(hm, hr, ga, gb, x2d, *consts)


def _block_diag_tiles(w):
    rows = w.reshape(M_INNER, M_QKV_BLOCK)
    wide = jnp.tile(rows, (1, BD_TILE // M_QKV_BLOCK))
    r_blk = (lax.broadcasted_iota(jnp.int32, wide.shape, 0) % BD_TILE) // M_QKV_BLOCK
    c_blk = lax.broadcasted_iota(jnp.int32, wide.shape, 1) // M_QKV_BLOCK
    t = jnp.where(r_blk == c_blk, wide, 0.0).astype(BF16)
    return t.reshape(M_INNER // BD_TILE, BD_TILE, BD_TILE)


def _rope_tables(pos0, n):
    half = R_QK_DIM // 2
    inv = ROPE_BASE ** (-np.arange(half, dtype=np.float64) / half)
    ang = (pos0 + np.arange(n, dtype=np.float64))[:, None] * inv[None, :]
    return jnp.asarray(np.cos(ang), F32), jnp.asarray(np.sin(ang), F32)


def kernel(x, meta_tokens, norm_mix_pre, w_in, conv_w, conv_b, w_q_m, w_k_m, w_v_m, w_if, b_if,
           skip_m, gn_m, gn_r, w_proj_m, w_proj_r, w_out, norm_mix_post, norm_ffn_pre, w_up,
           w_down, norm_ffn_post):
    nb, seq, _ = x.shape
    assert norm_mix_pre.shape[0] == 1, "single-layer block"
    assert seq % PROJ_TM == 0 and seq % PREP_TP == 0 and seq % (RET_STEP_CHUNKS * CHUNK_T) == 0
    row = lambda a: a[0].reshape(1, -1).astype(F32)
    n_pad = META_T - N_META

    w_in_bf = w_in[0].astype(BF16)
    g_pre = row(norm_mix_pre)
    cw, cb = conv_w[0].astype(F32), row(conv_b)

    meta_rows = jnp.concatenate(
        [jnp.zeros((n_pad, D_MODEL), x.dtype), meta_tokens.astype(x.dtype)], axis=0)
    cos_m, sin_m = _rope_tables(-n_pad, META_T)
    cos_x, sin_x = _rope_tables(N_META, seq)
    tail_zero = jnp.zeros((SUBLANES, M_INNER), F32)
    m_meta, r_meta, _ = _in_proj_all(meta_rows, g_pre, w_in_bf, cw, cb, tail_zero, cos_m, sin_m,
                                     META_T, 1, n_pad)
    x2d = x.reshape(nb * seq, D_MODEL)
    m_in, r_in, (ga, gb) = _in_proj_all(x2d, g_pre, w_in_bf, cw, cb, m_meta[3], cos_x, sin_x,
                                        PROJ_TM, seq // PROJ_TM, 0)

    nblk = M_INNER // M_QKV_BLOCK
    wif3 = w_if[0].reshape(3, nblk, M_QKV_BLOCK, 2 * M_HEADS)
    fold = lambda w, part: jnp.einsum('nio,noj->nij', w, part).reshape(M_INNER, 2 * M_HEADS)
    wg = jnp.concatenate([fold(w_q_m[0], wif3[0]) + fold(w_k_m[0], wif3[1]), fold(w_v_m[0], wif3[2])],
                         axis=0)
    wif = jnp.zeros((2 * M_INNER, GATE_PAD), F32).at[:, :2 * M_HEADS].set(wg).astype(BF16)
    bif = jnp.zeros((1, LANES), F32).at[0, :2 * M_HEADS].set(b_if[0])
    p_weights = (_block_diag_tiles(w_q_m[0]), _block_diag_tiles(w_k_m[0]),
                 _block_diag_tiles(w_v_m[0]), wif, bif)
    m_zero = (jnp.zeros((M_HEADS, M_HEAD_DIM, M_HEAD_DIM), F32),
              jnp.zeros((M_HEADS, SUBLANES, M_HEAD_DIM), F32),
              jnp.zeros((M_HEADS, SUBLANES, LANES), F32))
    as_seq = lambda a, n, t: a.reshape(n, t, a.shape[-1])
    xm_m, xc_m, zs_m = [as_seq(a, 1, META_T) for a in m_meta[:3]]
    xm_x, xc_x, zs_x = [as_seq(a, nb, seq) for a in m_in[:3]]
    prep_m = _mlstm_prep(xm_m, xc_m, p_weights, META_T, META_T, n_pad)
    m_init = _mlstm(*prep_m[:3], xc_m, zs_m, *prep_m[3:], row(skip_m), row(gn_m), m_zero,
                    META_T, True)[1:]
    prep_x = _mlstm_prep(xm_x, xc_x, p_weights, PREP_TP, CHUNK_T, 0)
    hm = _mlstm(*prep_x[:3], xc_x, zs_x, *prep_x[3:], row(skip_m), row(gn_m), tuple(m_init),
                CHUNK_T, False)[0]

    r_zero = jnp.zeros((R_HEADS, R_QK_DIM, R_V_DIM), F32)
    r_init = _retention(*[as_seq(a, 1, META_T) for a in r_meta], row(gn_r), r_zero,
                        META_T, True)[1]
    hr = _retention(*[as_seq(a, nb, seq) for a in r_in], row(gn_r), r_init, CHUNK_T, False)[0]

    out = _out_ffn(hm.reshape(nb * seq, M_INNER), hr.reshape(nb * seq, R_V), ga, gb, x2d,
                   w_proj_m[0].astype(BF16), w_proj_r[0].astype(BF16), w_out[0].astype(BF16),
                   w_up[0].astype(BF16), w_down[0].astype(BF16),
                   row(norm_mix_post), row(norm_ffn_pre), row(norm_ffn_post), OUT_TM)
    return out.reshape(nb, seq, D_MODEL)
```

```python
import functools
import math

import jax
import jax.numpy as jnp
import numpy as np
from jax import lax
from jax.experimental import pallas as pl
from jax.experimental.pallas import tpu as pltpu

F32 = jnp.float32
BF16 = jnp.bfloat16

D_MODEL = 1024
N_META = 16
M_INNER = 2 * D_MODEL
M_HEADS = 4
M_HEAD_DIM = M_INNER // M_HEADS
M_QKV_BLOCK = 4
M_CONV = 4
R_HEADS = 4
R_QK_DIM = D_MODEL // R_HEADS
R_V_DIM = 2 * R_QK_DIM
R_QK = R_HEADS * R_QK_DIM
R_V = R_HEADS * R_V_DIM
D_FF = 4 * D_MODEL
ROPE_BASE = 10000.0
EPS = 1e-6
NEG = -1e30
LOG2E = math.log2(math.e)
N_IN = 2 * M_INNER + 2 * R_QK + 2 * R_V + 2 * D_MODEL

LANES = 128
SUBLANES = 8
MXU_DIM = 256
VMEM_LIMIT_BYTES = 56 * 1024 * 1024

CHUNK_T = 256
META_T = 128
BD_TILE = MXU_DIM
GATE_PAD = MXU_DIM
MLSTM_STEP_CHUNKS = 1
RET_STEP_CHUNKS = 4
PREP_TP = 1024
CONV_STRIP = 512
PROJ_TM = 1024
PROJ_TM_CONV = 512
OUT_TM = 512
FF_TILE = 1024

_NT = (((1,), (1,)), ((), ()))
_TN = (((0,), (0,)), ((), ()))


def _const_spec(shape):
    nd = len(shape)
    return pl.BlockSpec(shape, lambda *_: (0,) * nd, pipeline_mode=pl.Buffered(1))


def _sigmoid(x):
    return 1.0 / (1.0 + jnp.exp2(x * (-LOG2E)))


def _silu(x):
    return x * _sigmoid(x)


def _normed(x_ref, g_ref):
    x = x_ref[...]
    ms = jnp.mean(x * x, axis=-1, keepdims=True)
    return (x * lax.rsqrt(ms + EPS) * g_ref[...]).astype(BF16)


def _inproj_mlstm_kernel(x_ref, g_ref, w_ref, wv_ref, wg_ref, cw_ref, cb_ref, t0_ref,
                         xm_ref, xc_ref, zs_ref, v_ref, gs_ref, to_ref, halo_s, res_s,
                         *, tm, tiles_per_seq):
    i = pl.program_id(0)

    @pl.when(i % tiles_per_seq == 0)
    def _():
        halo_s[...] = t0_ref[...]

    u = _normed(x_ref, g_ref)
    nstrip = M_INNER // CONV_STRIP
    nv = tm // SUBLANES
    sub = lax.broadcasted_iota(jnp.int32, (nv, SUBLANES, CONV_STRIP), 1)

    def strip_dot(c):
        sl = slice(c * CONV_STRIP, (c + 1) * CONV_STRIP)
        res_s[(i + c) % 2] = jnp.dot(u, w_ref[:, sl], preferred_element_type=F32)

    strip_dot(0)
    for c in range(nstrip):
        sl = slice(c * CONV_STRIP, (c + 1) * CONV_STRIP)
        if c + 1 < nstrip:
            strip_dot(c + 1)
        zsl = slice(M_INNER + c * CONV_STRIP, M_INNER + (c + 1) * CONV_STRIP)
        zm = jnp.dot(u, w_ref[:, zsl], preferred_element_type=F32)
        zs_ref[:, sl] = _silu(zm).astype(BF16)
        v_ref[:, sl] = jnp.dot(u, wv_ref[:, sl], preferred_element_type=F32).astype(BF16)
        gs_ref[:, sl] = _silu(jnp.dot(u, wg_ref[:, sl], preferred_element_type=F32)).astype(BF16)
        xm = res_s[(i + c) % 2]
        xm_ref[:, sl] = xm.astype(BF16)
        x3 = xm.reshape(nv, SUBLANES, CONV_STRIP)
        prev3 = jnp.concatenate([halo_s[:, sl].reshape(1, SUBLANES, CONV_STRIP), x3[:nv - 1]], axis=0)
        y = cb_ref[:, sl] + cw_ref[M_CONV - 1:M_CONV, sl] * x3
        for j in range(M_CONV - 1):
            sh = M_CONV - 1 - j
            merged = jnp.where(sub >= SUBLANES - sh, prev3, x3)
            y = y + cw_ref[j:j + 1, sl] * pltpu.roll(merged, sh, axis=1)
        halo_s[:, sl] = xm[tm - SUBLANES:, :]
        xc_ref[:, sl] = _silu(y.reshape(tm, CONV_STRIP)).astype(BF16)

    @pl.when(i == pl.num_programs(0) - 1)
    def _():
        to_ref[...] = halo_s[...]


def _inproj_rot_gate_kernel(x_ref, g_ref, w_ref, wmix_ref, cos_ref, sin_ref,
                            q_ref, k_ref, ga_ref, gb_ref, *, tm, n_pad):
    u = _normed(x_ref, g_ref)
    cos = cos_ref[...]
    sin = sin_ref[...]
    half = R_QK_DIM // 2
    kscale = R_QK_DIM ** -0.5
    if n_pad:
        kmul = (lax.broadcasted_iota(jnp.int32, (tm, 1), 0) >= n_pad).astype(F32) * kscale

    def rot(t, h):
        t1 = t[:, h * R_QK_DIM:h * R_QK_DIM + half]
        t2 = t[:, h * R_QK_DIM + half:(h + 1) * R_QK_DIM]
        return t1 * cos - t2 * sin, t1 * sin + t2 * cos

    q = jnp.dot(u, w_ref[:, 0:R_QK], preferred_element_type=F32)
    for h in range(R_HEADS):
        a, b = rot(q, h)
        q_ref[:, h * R_QK_DIM:h * R_QK_DIM + half] = a.astype(BF16)
        q_ref[:, h * R_QK_DIM + half:(h + 1) * R_QK_DIM] = b.astype(BF16)
    k = jnp.dot(u, w_ref[:, R_QK:2 * R_QK], preferred_element_type=F32)
    for h in range(R_HEADS):
        a, b = rot(k, h)
        if n_pad:
            a, b = a * kmul, b * kmul
        else:
            a, b = a * kscale, b * kscale
        k_ref[:, h * R_QK_DIM:h * R_QK_DIM + half] = a.astype(BF16)
        k_ref[:, h * R_QK_DIM + half:(h + 1) * R_QK_DIM] = b.astype(BF16)
    ga_ref[...] = _sigmoid(jnp.dot(u, wmix_ref[:, 0:D_MODEL], preferred_element_type=F32)).astype(BF16)
    gb_ref[...] = _sigmoid(
        jnp.dot(u, wmix_ref[:, D_MODEL:2 * D_MODEL], preferred_element_type=F32)).astype(BF16)


def _w_cols(width, start):
    assert start % width == 0
    return pl.BlockSpec((D_MODEL, width), lambda *_: (0, start // width), pipeline_mode=pl.Buffered(1))


def _in_proj_call(body, name, x2d, gain, w_bf, w_specs, extra_in, extra_specs, out_widths, tm,
                  extra_out=(), extra_out_specs=(), scratch=()):
    rows = x2d.shape[0]
    row_spec = lambda w: pl.BlockSpec((tm, w), lambda i: (i, 0))
    return pl.pallas_call(
        body,
        grid=(rows // tm,),
        in_specs=[row_spec(D_MODEL), _const_spec(gain.shape), *w_specs, *extra_specs],
        out_specs=[*[row_spec(w) for w in out_widths], *extra_out_specs],
        out_shape=[*[jax.ShapeDtypeStruct((rows, w), BF16) for w in out_widths], *extra_out],
        scratch_shapes=list(scratch),
        compiler_params=pltpu.CompilerParams(
            dimension_semantics=("arbitrary",),
            vmem_limit_bytes=VMEM_LIMIT_BYTES),
        name=name,
    )(x2d, gain, *([w_bf] * len(w_specs)), *extra_in)


def _in_proj_all(x2d, gain, w_bf, cw, cb, tail0, cos, sin, tm, tiles_per_seq, n_pad):
    c_qk = 2 * M_INNER
    c_v = c_qk + 2 * R_QK
    c_g = c_v + R_V
    c_mix = c_g + R_V
    tail_shape = (SUBLANES, M_INNER)
    tmc = min(tm, PROJ_TM_CONV)
    xm, xc, zs, v, gs, tail = _in_proj_call(
        functools.partial(_inproj_mlstm_kernel, tm=tmc, tiles_per_seq=tiles_per_seq * tm // tmc),
        "in_proj_mlstm", x2d, gain, w_bf,
        [_w_cols(2 * M_INNER, 0), _w_cols(R_V, c_v), _w_cols(R_V, c_g)], (cw, cb, tail0),
        (_const_spec(cw.shape), _const_spec(cb.shape), _const_spec(tail_shape)),
        (M_INNER, M_INNER, M_INNER, R_V, R_V), tmc,
        extra_out=(jax.ShapeDtypeStruct(tail_shape, F32),),
        extra_out_specs=(pl.BlockSpec(tail_shape, lambda i: (0, 0)),),
        scratch=(pltpu.VMEM(tail_shape, F32), pltpu.VMEM((2, tmc, CONV_STRIP), F32)))
    pos_spec = pl.BlockSpec((tm, R_QK_DIM // 2), lambda i: (i % tiles_per_seq, 0))
    qr, kr, ga, gb = _in_proj_call(
        functools.partial(_inproj_rot_gate_kernel, tm=tm, n_pad=n_pad),
        "in_proj_rot_gate", x2d, gain, w_bf, [_w_cols(2 * R_QK, c_qk), _w_cols(2 * D_MODEL, c_mix)],
        (cos, sin), (pos_spec, pos_spec), (R_QK, R_QK, D_MODEL, D_MODEL), tm)
    return (xm, xc, zs, tail), (qr, kr, v, gs), (ga, gb)


def _head_norm(h, w):
    mu = jnp.mean(h, axis=-1, keepdims=True)
    c = h - mu
    var = jnp.mean(c * c, axis=-1, keepdims=True)
    return c * lax.rsqrt(var + EPS) * w


def _lane_cumsum(x, n):
    lane = lax.broadcasted_iota(jnp.int32, x.shape, 1)
    d = 1
    while d < n:
        x = x + jnp.where(lane >= d, pltpu.roll(x, d, axis=1), 0.0)
        d *= 2
    return x


def _mlstm_prep_kernel(xm_ref, xc_ref, bdq_ref, bdk_ref, bdv_ref, wif_ref, bif_ref,
                       q_ref, k_ref, v_ref, row_ref, col_ref, *, TP, T, n_pad):
    scale = M_HEAD_DIM ** -0.5
    for j in range(M_INNER // BD_TILE):
        sl = slice(j * BD_TILE, (j + 1) * BD_TILE)
        q_ref[:, sl] = jnp.dot(xc_ref[:, sl], bdq_ref[j], preferred_element_type=F32).astype(BF16)
        kj = jnp.dot(xc_ref[:, sl], bdk_ref[j], preferred_element_type=F32)
        k_ref[:, sl] = (kj * scale).astype(BF16)
        v_ref[:, sl] = jnp.dot(xm_ref[:, sl], bdv_ref[j], preferred_element_type=F32).astype(BF16)
    xcm = jnp.concatenate([xc_ref[...], xm_ref[...]], axis=1)
    g_all = jnp.dot(xcm, wif_ref[...], preferred_element_type=F32)[:, 0:LANES] + bif_ref[...]
    lane = lax.broadcasted_iota(jnp.int32, (T, LANES), 1)
    subl = lax.broadcasted_iota(jnp.int32, (SUBLANES, T), 0)
    for c in range(TP // T):
        g = g_all[c * T:(c + 1) * T, :]
        if n_pad:
            rowv = lax.broadcasted_iota(jnp.int32, (T, LANES), 0) >= n_pad
            g = jnp.where(jnp.logical_and(lane < M_HEADS, jnp.logical_not(rowv)), NEG, g)
        gt = g.T
        lf_t = jnp.minimum(gt, 0.0) - jnp.log1p(jnp.exp(-jnp.abs(gt)))
        cum_t = _lane_cumsum(lf_t, T)
        row_ref[c] = jnp.where(subl < M_HEADS, gt[0:SUBLANES, :], cum_t[0:SUBLANES, :])
        col_ref[c * T:(c + 1) * T, :] = jnp.where(lane < M_HEADS, g, cum_t.T)


def _mlstm_prep(xm, xc, weights, TP, T, n_pad):
    nb, L, _ = xm.shape
    blk = pl.BlockSpec((None, TP, M_INNER), lambda b, i: (b, i, 0))
    big = jax.ShapeDtypeStruct((nb, L, M_INNER), BF16)
    return pl.pallas_call(
        functools.partial(_mlstm_prep_kernel, TP=TP, T=T, n_pad=n_pad),
        grid=(nb, L // TP),
        in_specs=[blk, blk, *[_const_spec(w.shape) for w in weights]],
        out_specs=[blk, blk, blk,
                   pl.BlockSpec((None, TP // T, SUBLANES, T), lambda b, i: (b, i, 0, 0)),
                   pl.BlockSpec((None, TP, LANES), lambda b, i: (b, i, 0))],
        out_shape=[big, big, big,
                   jax.ShapeDtypeStruct((nb, L // T, SUBLANES, T), F32),
                   jax.ShapeDtypeStruct((nb, L, LANES), F32)],
        compiler_params=pltpu.CompilerParams(
            dimension_semantics=("arbitrary", "arbitrary"),
            vmem_limit_bytes=VMEM_LIMIT_BYTES),
        name="mlstm_prep",
    )(xm, xc, *weights)


def _mlstm_kernel(q_ref, k_ref, v_ref, xc_ref, zs_ref, row_ref, col_ref, skip_ref, gn_ref,
                  c0_ref, n0_ref, m0_ref, h_ref, *rest, T, nsub, emit_state):
    if emit_state:
        co_ref, no_ref, mo_ref, c_s, cb_s, n_s, m_s = rest
    else:
        c_s, cb_s, n_s, m_s = rest
    ci = pl.program_id(1)
    nc = pl.num_programs(1)

    @pl.when(ci == 0)
    def _():
        c_s[...] = c0_ref[...]
        cb_s[...] = c0_ref[...].astype(BF16)
        n_s[...] = n0_ref[...]
        m_s[...] = m0_ref[...]

    causal = (lax.broadcasted_iota(jnp.int32, (T, T), 0)
              >= lax.broadcasted_iota(jnp.int32, (T, T), 1))
    heads = range(M_HEADS)
    sls = [slice(h * M_HEAD_DIM, (h + 1) * M_HEAD_DIM) for h in heads]

    def chunk(rows, rowp):
        colp = col_ref[rows, :]

        s_raw = [lax.dot_general(q_ref[rows, sls[h]], k_ref[rows, sls[h]], _NT,
                                 preferred_element_type=F32) for h in heads]
        qc = [jnp.dot(q_ref[rows, sls[h]], cb_s[h], preferred_element_type=F32) for h in heads]

        dmat, w_inter, e_negm, w_k, decay, m_new = [], [], [], [], [], []
        for h in heads:
            i_row = rowp[h:h + 1, :]
            f_row = rowp[M_HEADS + h:M_HEADS + h + 1, :]
            i_col = colp[:, h:h + 1]
            f_col = colp[:, M_HEADS + h:M_HEADS + h + 1]
            g_tot = f_row[:, T - 1:T]
            m_prev = m_s[h][0:1, 0:1]
            b_row = i_row - f_row
            log_d = jnp.where(causal, f_col + b_row, NEG)
            inter = f_col + m_prev
            m_row = jnp.maximum(inter, jnp.max(log_d, axis=1, keepdims=True))
            dmat.append(jnp.exp(log_d - m_row))
            w_inter.append(jnp.exp(inter - m_row))
            e_negm.append(jnp.exp(-m_row))
            mn = jnp.maximum(g_tot + m_prev, jnp.max(g_tot + b_row, axis=1, keepdims=True))
            m_new.append(mn)
            w_k.append(jnp.exp(g_tot - f_col + i_col - mn))
            decay.append(jnp.exp(g_tot + m_prev - mn))

        hhs = []
        for h in heads:
            qn = jnp.sum(q_ref[rows, sls[h]].astype(F32) * n_s[h][0:1, :], axis=1, keepdims=True)
            s = s_raw[h] * dmat[h]
            num = (jnp.dot(s.astype(BF16), v_ref[rows, sls[h]], preferred_element_type=F32)
                   + w_inter[h] * qc[h])
            den = jnp.sum(s, axis=1, keepdims=True) + w_inter[h] * qn
            hhs.append(num * (1.0 / jnp.maximum(jnp.abs(den), e_negm[h])))
        for h in heads:
            kw = k_ref[rows, sls[h]].astype(F32) * w_k[h]
            c_new = decay[h] * c_s[h] + lax.dot_general(kw.astype(BF16), v_ref[rows, sls[h]], _TN,
                                                        preferred_element_type=F32)
            n_new = decay[h] * n_s[h][0:1, :] + jnp.sum(kw, axis=0, keepdims=True)
            c_s[h] = c_new
            cb_s[h] = c_new.astype(BF16)
            n_s[h] = jnp.broadcast_to(n_new, (SUBLANES, M_HEAD_DIM))
            m_s[h] = jnp.broadcast_to(m_new[h], (SUBLANES, LANES))
        for h in heads:
            hn = _head_norm(hhs[h], gn_ref[:, sls[h]])
            out = ((hn + skip_ref[:, sls[h]] * xc_ref[rows, sls[h]].astype(F32))
                   * zs_ref[rows, sls[h]].astype(F32))
            h_ref[rows, sls[h]] = out.astype(BF16)

    for cc in range(nsub):
        chunk(slice(cc * T, (cc + 1) * T), row_ref[cc])

    if emit_state:
        @pl.when(ci == nc - 1)
        def _():
            co_ref[...] = c_s[...]
            no_ref[...] = n_s[...]
            mo_ref[...] = m_s[...]


def _mlstm(q, k, v, xc, zs, rowp, colp, skip, gn, state, T, emit_state):
    nb, L, _ = q.shape
    nsub = min(MLSTM_STEP_CHUNKS, L // T)
    nc = L // (T * nsub)
    blk = pl.BlockSpec((None, T * nsub, M_INNER), lambda b, c: (b, c, 0))
    in_specs = [blk, blk, blk, blk, blk,
                pl.BlockSpec((None, nsub, SUBLANES, T), lambda b, c: (b, c, 0, 0)),
                pl.BlockSpec((None, T * nsub, LANES), lambda b, c: (b, c, 0)),
                _const_spec(skip.shape), _const_spec(gn.shape),
                *[_const_spec(s.shape) for s in state]]
    out_shape = [jax.ShapeDtypeStruct((nb, L, M_INNER), BF16)]
    out_specs = [blk]
    if emit_state:
        assert nb == 1
        for s in state:
            out_shape.append(jax.ShapeDtypeStruct(s.shape, F32))
            out_specs.append(pl.BlockSpec(s.shape, lambda b, c, _n=len(s.shape): (0,) * _n))
    c_shape = state[0].shape
    scratch = [pltpu.VMEM(c_shape, F32), pltpu.VMEM(c_shape, BF16),
               pltpu.VMEM(state[1].shape, F32), pltpu.VMEM(state[2].shape, F32)]
    return pl.pallas_call(
        functools.partial(_mlstm_kernel, T=T, nsub=nsub, emit_state=emit_state),
        grid=(nb, nc),
        in_specs=in_specs,
        out_specs=out_specs,
        out_shape=out_shape,
        scratch_shapes=scratch,
        compiler_params=pltpu.CompilerParams(
            dimension_semantics=("arbitrary", "arbitrary"),
            vmem_limit_bytes=VMEM_LIMIT_BYTES),
        name="mlstm_state" if emit_state else "mlstm",
    )(q, k, v, xc, zs, rowp, colp, skip, gn, *state)


def _retention_kernel(q_ref, k_ref, v_ref, gs_ref, gn_ref, s0_ref,
                      o_ref, *rest, T, nsub, emit_state):
    if emit_state:
        so_ref, st_s, dm_s = rest
    else:
        st_s, dm_s = rest
    ci = pl.program_id(1)
    nc = pl.num_programs(1)
    log_gamma = [math.log(1.0 - 2.0 ** (-5.0 - h)) for h in range(R_HEADS)]

    @pl.when(ci == 0)
    def _():
        st_s[...] = s0_ref[...]
        diff = (lax.broadcasted_iota(jnp.int32, (T, T), 0)
                - lax.broadcasted_iota(jnp.int32, (T, T), 1))
        dfl = diff.astype(F32)
        for h in range(R_HEADS):
            dm_s[h] = jnp.where(diff >= 0, jnp.exp(log_gamma[h] * dfl), 0.0)

    tf = lax.broadcasted_iota(jnp.int32, (T, 1), 0).astype(F32)
    for cc in range(nsub):
        rows = slice(cc * T, (cc + 1) * T)
        for h in range(R_HEADS):
            qh = q_ref[rows, h * R_QK_DIM:(h + 1) * R_QK_DIM]
            kh = k_ref[rows, h * R_QK_DIM:(h + 1) * R_QK_DIM]
            sl = slice(h * R_V_DIM, (h + 1) * R_V_DIM)
            vh = v_ref[rows, sl]
            s = lax.dot_general(qh, kh, _NT, preferred_element_type=F32) * dm_s[h]
            q_decay = jnp.exp(log_gamma[h] * (tf + 1.0))
            k_decay = jnp.exp(log_gamma[h] * (T - 1.0 - tf))
            o = (jnp.dot(s.astype(BF16), vh, preferred_element_type=F32)
                 + q_decay * jnp.dot(qh, st_s[h].astype(BF16), preferred_element_type=F32))
            kd = (kh.astype(F32) * k_decay).astype(BF16)
            st_s[h] = (math.exp(log_gamma[h] * T) * st_s[h]
                       + lax.dot_general(kd, vh, _TN, preferred_element_type=F32))
            o_ref[rows, sl] = (_head_norm(o, gn_ref[:, sl])
                               * gs_ref[rows, sl].astype(F32)).astype(BF16)

    if emit_state:
        @pl.when(ci == nc - 1)
        def _():
            so_ref[...] = st_s[...]


def _retention(q, k, v, gs, gn, s0, T, emit_state):
    nb, L, _ = q.shape
    nsub = min(RET_STEP_CHUNKS, L // T)
    nc = L // (T * nsub)
    qk_blk = pl.BlockSpec((None, T * nsub, R_QK), lambda b, c: (b, c, 0))
    v_blk = pl.BlockSpec((None, T * nsub, R_V), lambda b, c: (b, c, 0))
    in_specs = [qk_blk, qk_blk, v_blk, v_blk, _const_spec(gn.shape), _const_spec(s0.shape)]
    out_shape = [jax.ShapeDtypeStruct((nb, L, R_V), BF16)]
    out_specs = [v_blk]
    if emit_state:
        assert nb == 1
        out_shape.append(jax.ShapeDtypeStruct(s0.shape, F32))
        out_specs.append(pl.BlockSpec(s0.shape, lambda b, c: (0, 0, 0)))
    scratch = [pltpu.VMEM(s0.shape, F32), pltpu.VMEM((R_HEADS, T, T), F32)]
    return pl.pallas_call(
        functools.partial(_retention_kernel, T=T, nsub=nsub, emit_state=emit_state),
        grid=(nb, nc),
        in_specs=in_specs,
        out_specs=out_specs,
        out_shape=out_shape,
        scratch_shapes=scratch,
        compiler_params=pltpu.CompilerParams(
            dimension_semantics=("arbitrary", "arbitrary"),
            vmem_limit_bytes=VMEM_LIMIT_BYTES),
        name="retention_state" if emit_state else "retention",
    )(q, k, v, gs, gn, s0)


def _rms(x, g):
    return x * lax.rsqrt(jnp.mean(x * x, axis=-1, keepdims=True) + EPS) * g


def _out_ffn_kernel(hm_ref, hr_ref, ga_ref, gb_ref, x_ref, wpm_ref, wpr_ref, wout_ref,
                    wup_ref, wdn_ref, g1_ref, g2_ref, g3_ref, o_ref):
    ya = jnp.dot(hm_ref[...], wpm_ref[...], preferred_element_type=F32)
    yb = jnp.dot(hr_ref[...], wpr_ref[...], preferred_element_type=F32)
    mixin = ga_ref[...].astype(F32) * ya + gb_ref[...].astype(F32) * yb
    mix = jnp.dot(mixin.astype(BF16), wout_ref[...], preferred_element_type=F32)
    h1 = x_ref[...] + _rms(mix, g1_ref[...])
    u = _rms(h1, g2_ref[...]).astype(BF16)
    f = None
    for j in range(D_FF // FF_TILE):
        sl = slice(j * FF_TILE, (j + 1) * FF_TILE)
        a = jnp.maximum(jnp.dot(u, wup_ref[:, sl], preferred_element_type=F32), 0.0)
        part = jnp.dot((a * a).astype(BF16), wdn_ref[sl, :], preferred_element_type=F32)
        f = part if f is None else f + part
    o_ref[...] = h1 + _rms(f, g3_ref[...])


def _out_ffn(hm, hr, ga, gb, x2d, wpm, wpr, wout, wup, wdn, g1, g2, g3, tm):
    rows = x2d.shape[0]
    row_spec = lambda w: pl.BlockSpec((tm, w), lambda i: (i, 0))
    consts = (wpm, wpr, wout, wup, wdn, g1, g2, g3)
    return pl.pallas_call(
        _out_ffn_kernel,
        grid=(rows // tm,),
        in_specs=[row_spec(M_INNER), row_spec(R_V), row_spec(D_MODEL), row_spec(D_MODEL),
                  row_spec(D_MODEL), *[_const_spec(c.shape) for c in consts]],
        out_specs=row_spec(D_MODEL),
        out_shape=jax.ShapeDtypeStruct((rows, D_MODEL), F32),
        compiler_params=pltpu.CompilerParams(
            dimension_semantics=("arbitrary",),
            vmem_limit_bytes=VMEM_LIMIT_BYTES),
        name="out_ffn",
    )(hm, hr, ga, gb, x2d, *consts)


def _block_diag_tiles(w):
    rows = w.reshape(M_INNER, M_QKV_BLOCK)
    wide = jnp.tile(rows, (1, BD_TILE // M_QKV_BLOCK))
    r_blk = (lax.broadcasted_iota(jnp.int32, wide.shape, 0) % BD_TILE) // M_QKV_BLOCK
    c_blk = lax.broadcasted_iota(jnp.int32, wide.shape, 1) // M_QKV_BLOCK
    t = jnp.where(r_blk == c_blk, wide, 0.0).astype(BF16)
    return t.reshape(M_INNER // BD_TILE, BD_TILE, BD_TILE)


def _rope_tables(pos0, n):
    half = R_QK_DIM // 2
    inv = ROPE_BASE ** (-np.arange(half, dtype=np.float64) / half)
    ang = (pos0 + np.arange(n, dtype=np.float64))[:, None] * inv[None, :]
    return jnp.asarray(np.cos(ang), F32), jnp.asarray(np.sin(ang), F32)


def kernel(x, meta_tokens, norm_mix_pre, w_in, conv_w, conv_b, w_q_m, w_k_m, w_v_m, w_if, b_if,
           skip_m, gn_m, gn_r, w_proj_m, w_proj_r, w_out, norm_mix_post, norm_ffn_pre, w_up,
           w_down, norm_ffn_post):
    nb, seq, _ = x.shape
    assert norm_mix_pre.shape[0] == 1, "single-layer block"
    assert seq % PROJ_TM == 0 and seq % PREP_TP == 0 and seq % (RET_STEP_CHUNKS * CHUNK_T) == 0
    row = lambda a: a[0].reshape(1, -1).astype(F32)
    n_pad = META_T - N_META

    w_in_bf = w_in[0].astype(BF16)
    g_pre = row(norm_mix_pre)
    cw, cb = conv_w[0].astype(F32), row(conv_b)

    meta_rows = jnp.concatenate(
        [jnp.zeros((n_pad, D_MODEL), x.dtype), meta_tokens.astype(x.dtype)], axis=0)
    cos_m, sin_m = _rope_tables(-n_pad, META_T)
    cos_x, sin_x = _rope_tables(N_META, seq)
    tail_zero = jnp.zeros((SUBLANES, M_INNER), F32)
    m_meta, r_meta, _ = _in_proj_all(meta_rows, g_pre, w_in_bf, cw, cb, tail_zero, cos_m, sin_m,
                                     META_T, 1, n_pad)
    x2d = x.reshape(nb * seq, D_MODEL)
    m_in, r_in, (ga, gb) = _in_proj_all(x2d, g_pre, w_in_bf, cw, cb, m_meta[3], cos_x, sin_x,
                                        PROJ_TM, seq // PROJ_TM, 0)

    nblk = M_INNER // M_QKV_BLOCK
    wif3 = w_if[0].reshape(3, nblk, M_QKV_BLOCK, 2 * M_HEADS)
    fold = lambda w, part: jnp.einsum('nio,noj->nij', w, part).reshape(M_INNER, 2 * M_HEADS)
    wg = jnp.concatenate([fold(w_q_m[0], wif3[0]) + fold(w_k_m[0], wif3[1]), fold(w_v_m[0], wif3[2])],
                         axis=0)
    wif = jnp.zeros((2 * M_INNER, GATE_PAD), F32).at[:, :2 * M_HEADS].set(wg).astype(BF16)
    bif = jnp.zeros((1, LANES), F32).at[0, :2 * M_HEADS].set(b_if[0])
    p_weights = (_block_diag_tiles(w_q_m[0]), _block_diag_tiles(w_k_m[0]),
                 _block_diag_tiles(w_v_m[0]), wif, bif)
    m_zero = (jnp.zeros((M_HEADS, M_HEAD_DIM, M_HEAD_DIM), F32),
              jnp.zeros((M_HEADS, SUBLANES, M_HEAD_DIM), F32),
              jnp.zeros((M_HEADS, SUBLANES, LANES), F32))
    as_seq = lambda a, n, t: a.reshape(n, t, a.shape[-1])
    xm_m, xc_m, zs_m = [as_seq(a, 1, META_T) for a in m_meta[:3]]
    xm_x, xc_x, zs_x = [as_seq(a, nb, seq) for a in m_in[:3]]
    prep_m = _mlstm_prep(xm_m, xc_m, p_weights, META_T, META_T, n_pad)
    m_init = _mlstm(*prep_m[:3], xc_m, zs_m, *prep_m[3:], row(skip_m), row(gn_m), m_zero,
                    META_T, True)[1:]
    prep_x = _mlstm_prep(xm_x, xc_x, p_weights, PREP_TP, CHUNK_T, 0)
    hm = _mlstm(*prep_x[:3], xc_x, zs_x, *prep_x[3:], row(skip_m), row(gn_m), tuple(m_init),
                CHUNK_T, False)[0]

    r_zero = jnp.zeros((R_HEADS, R_QK_DIM, R_V_DIM), F32)
    r_init = _retention(*[as_seq(a, 1, META_T) for a in r_meta], row(gn_r), r_zero,
                        META_T, True)[1]
    hr = _retention(*[as_seq(a, nb, seq) for a in r_in], row(gn_r), r_init, CHUNK_T, False)[0]

    out = _out_ffn(hm.reshape(nb * seq, M_INNER), hr.reshape(nb * seq, R_V), ga, gb, x2d,
                   w_proj_m[0].astype(BF16), w_proj_r[0].astype(BF16), w_out[0].astype(BF16),
                   w_up[0].astype(BF16), w_down[0].astype(BF16),
                   row(norm_mix_post), row(norm_ffn_pre), row(norm_ffn_post), OUT_TM)
    return out.reshape(nb, seq, D_MODEL)
```

```python
import functools
import math

import jax
import jax.numpy as jnp
import numpy as np
from jax import lax
from jax.experimental import pallas as pl
from jax.experimental.pallas import tpu as pltpu

F32 = jnp.float32
BF16 = jnp.bfloat16

D_MODEL = 1024
N_META = 16
M_INNER = 2 * D_MODEL
M_HEADS = 4
M_HEAD_DIM = M_INNER // M_HEADS
M_QKV_BLOCK = 4
M_CONV = 4
R_HEADS = 4
R_QK_DIM = D_MODEL // R_HEADS
R_V_DIM = 2 * R_QK_DIM
R_QK = R_HEADS * R_QK_DIM
R_V = R_HEADS * R_V_DIM
D_FF = 4 * D_MODEL
ROPE_BASE = 10000.0
EPS = 1e-6
NEG = -1e30
LOG2E = math.log2(math.e)
N_IN = 2 * M_INNER + 2 * R_QK + 2 * R_V + 2 * D_MODEL

LANES = 128
SUBLANES = 8
MXU_DIM = 256
VMEM_LIMIT_BYTES = 56 * 1024 * 1024
MLSTM_VMEM_LIMIT_BYTES = 32 * 1024 * 1024

CHUNK_T = 256
META_T = 128
BD_TILE = MXU_DIM
GATE_PAD = MXU_DIM
MLSTM_STEP_CHUNKS = 1
RET_STEP_CHUNKS = 4
PREP_TP = 1024
CONV_STRIP = 512
PROJ_TM = 1024
PROJ_TM_CONV = 512
OUT_TM = 512
FF_TILE = 1024

_NT = (((1,), (1,)), ((), ()))
_TN = (((0,), (0,)), ((), ()))


def _const_spec(shape):
    nd = len(shape)
    return pl.BlockSpec(shape, lambda *_: (0,) * nd, pipeline_mode=pl.Buffered(1))


def _sigmoid(x):
    return 1.0 / (1.0 + jnp.exp2(x * (-LOG2E)))


def _silu(x):
    return x * _sigmoid(x)


def _normed(x_ref, g_ref):
    x = x_ref[...]
    ms = jnp.mean(x * x, axis=-1, keepdims=True)
    return (x * lax.rsqrt(ms + EPS) * g_ref[...]).astype(BF16)


def _inproj_mlstm_kernel(x_ref, g_ref, w_ref, wv_ref, wg_ref, cw_ref, cb_ref, t0_ref,
                         xm_ref, xc_ref, zs_ref, v_ref, gs_ref, to_ref, halo_s, res_s,
                         *, tm, tiles_per_seq):
    i = pl.program_id(0)

    @pl.when(i % tiles_per_seq == 0)
    def _():
        halo_s[...] = t0_ref[...]

    u = _normed(x_ref, g_ref)
    nstrip = M_INNER // CONV_STRIP
    nv = tm // SUBLANES
    sub = lax.broadcasted_iota(jnp.int32, (nv, SUBLANES, CONV_STRIP), 1)

    def strip_dot(c):
        sl = slice(c * CONV_STRIP, (c + 1) * CONV_STRIP)
        res_s[(i + c) % 2] = jnp.dot(u, w_ref[:, sl], preferred_element_type=F32)

    strip_dot(0)
    for c in range(nstrip):
        sl = slice(c * CONV_STRIP, (c + 1) * CONV_STRIP)
        if c + 1 < nstrip:
            strip_dot(c + 1)
        zsl = slice(M_INNER + c * CONV_STRIP, M_INNER + (c + 1) * CONV_STRIP)
        zm = jnp.dot(u, w_ref[:, zsl], preferred_element_type=F32)
        zs_ref[:, sl] = _silu(zm).astype(BF16)
        v_ref[:, sl] = jnp.dot(u, wv_ref[:, sl], preferred_element_type=F32).astype(BF16)
        gs_ref[:, sl] = _silu(jnp.dot(u, wg_ref[:, sl], preferred_element_type=F32)).astype(BF16)
        xm = res_s[(i + c) % 2]
        xm_ref[:, sl] = xm.astype(BF16)
        x3 = xm.reshape(nv, SUBLANES, CONV_STRIP)
        prev3 = jnp.concatenate([halo_s[:, sl].reshape(1, SUBLANES, CONV_STRIP), x3[:nv - 1]], axis=0)
        y = cb_ref[:, sl] + cw_ref[M_CONV - 1:M_CONV, sl] * x3
        for j in range(M_CONV - 1):
            sh = M_CONV - 1 - j
            merged = jnp.where(sub >= SUBLANES - sh, prev3, x3)
            y = y + cw_ref[j:j + 1, sl] * pltpu.roll(merged, sh, axis=1)
        halo_s[:, sl] = xm[tm - SUBLANES:, :]
        xc_ref[:, sl] = _silu(y.reshape(tm, CONV_STRIP)).astype(BF16)

    @pl.when(i == pl.num_programs(0) - 1)
    def _():
        to_ref[...] = halo_s[...]


def _inproj_rot_gate_kernel(x_ref, g_ref, w_ref, wmix_ref, cos_ref, sin_ref,
                            q_ref, k_ref, ga_ref, gb_ref, *, tm, n_pad):
    u = _normed(x_ref, g_ref)
    cos = cos_ref[...]
    sin = sin_ref[...]
    half = R_QK_DIM // 2
    kscale = R_QK_DIM ** -0.5
    if n_pad:
        kmul = (lax.broadcasted_iota(jnp.int32, (tm, 1), 0) >= n_pad).astype(F32) * kscale

    def rot(t, h):
        t1 = t[:, h * R_QK_DIM:h * R_QK_DIM + half]
        t2 = t[:, h * R_QK_DIM + half:(h + 1) * R_QK_DIM]
        return t1 * cos - t2 * sin, t1 * sin + t2 * cos

    q = jnp.dot(u, w_ref[:, 0:R_QK], preferred_element_type=F32)
    for h in range(R_HEADS):
        a, b = rot(q, h)
        q_ref[:, h * R_QK_DIM:h * R_QK_DIM + half] = a.astype(BF16)
        q_ref[:, h * R_QK_DIM + half:(h + 1) * R_QK_DIM] = b.astype(BF16)
    k = jnp.dot(u, w_ref[:, R_QK:2 * R_QK], preferred_element_type=F32)
    for h in range(R_HEADS):
        a, b = rot(k, h)
        if n_pad:
            a, b = a * kmul, b * kmul
        else:
            a, b = a * kscale, b * kscale
        k_ref[:, h * R_QK_DIM:h * R_QK_DIM + half] = a.astype(BF16)
        k_ref[:, h * R_QK_DIM + half:(h + 1) * R_QK_DIM] = b.astype(BF16)
    ga_ref[...] = _sigmoid(jnp.dot(u, wmix_ref[:, 0:D_MODEL], preferred_element_type=F32)).astype(BF16)
    gb_ref[...] = _sigmoid(
        jnp.dot(u, wmix_ref[:, D_MODEL:2 * D_MODEL], preferred_element_type=F32)).astype(BF16)


def _w_cols(width, start):
    assert start % width == 0
    return pl.BlockSpec((D_MODEL, width), lambda *_: (0, start // width), pipeline_mode=pl.Buffered(1))


def _in_proj_call(body, name, x2d, gain, w_bf, w_specs, extra_in, extra_specs, out_widths, tm,
                  extra_out=(), extra_out_specs=(), scratch=()):
    rows = x2d.shape[0]
    row_spec = lambda w: pl.BlockSpec((tm, w), lambda i: (i, 0))
    return pl.pallas_call(
        body,
        grid=(rows // tm,),
        in_specs=[row_spec(D_MODEL), _const_spec(gain.shape), *w_specs, *extra_specs],
        out_specs=[*[row_spec(w) for w in out_widths], *extra_out_specs],
        out_shape=[*[jax.ShapeDtypeStruct((rows, w), BF16) for w in out_widths], *extra_out],
        scratch_shapes=list(scratch),
        compiler_params=pltpu.CompilerParams(
            dimension_semantics=("arbitrary",),
            vmem_limit_bytes=VMEM_LIMIT_BYTES),
        name=name,
    )(x2d, gain, *([w_bf] * len(w_specs)), *extra_in)


def _in_proj_all(x2d, gain, w_bf, cw, cb, tail0, cos, sin, tm, tiles_per_seq, n_pad):
    c_qk = 2 * M_INNER
    c_v = c_qk + 2 * R_QK
    c_g = c_v + R_V
    c_mix = c_g + R_V
    tail_shape = (SUBLANES, M_INNER)
    tmc = min(tm, PROJ_TM_CONV)
    xm, xc, zs, v, gs, tail = _in_proj_call(
        functools.partial(_inproj_mlstm_kernel, tm=tmc, tiles_per_seq=tiles_per_seq * tm // tmc),
        "in_proj_mlstm", x2d, gain, w_bf,
        [_w_cols(2 * M_INNER, 0), _w_cols(R_V, c_v), _w_cols(R_V, c_g)], (cw, cb, tail0),
        (_const_spec(cw.shape), _const_spec(cb.shape), _const_spec(tail_shape)),
        (M_INNER, M_INNER, M_INNER, R_V, R_V), tmc,
        extra_out=(jax.ShapeDtypeStruct(tail_shape, F32),),
        extra_out_specs=(pl.BlockSpec(tail_shape, lambda i: (0, 0)),),
        scratch=(pltpu.VMEM(tail_shape, F32), pltpu.VMEM((2, tmc, CONV_STRIP), F32)))
    pos_spec = pl.BlockSpec((tm, R_QK_DIM // 2), lambda i: (i % tiles_per_seq, 0))
    qr, kr, ga, gb = _in_proj_call(
        functools.partial(_inproj_rot_gate_kernel, tm=tm, n_pad=n_pad),
        "in_proj_rot_gate", x2d, gain, w_bf, [_w_cols(2 * R_QK, c_qk), _w_cols(2 * D_MODEL, c_mix)],
        (cos, sin), (pos_spec, pos_spec), (R_QK, R_QK, D_MODEL, D_MODEL), tm)
    return (xm, xc, zs, tail), (qr, kr, v, gs), (ga, gb)


def _head_norm(h, w):
    mu = jnp.mean(h, axis=-1, keepdims=True)
    c = h - mu
    var = jnp.mean(c * c, axis=-1, keepdims=True)
    return c * lax.rsqrt(var + EPS) * w


def _lane_cumsum(x, n):
    lane = lax.broadcasted_iota(jnp.int32, x.shape, 1)
    d = 1
    while d < n:
        x = x + jnp.where(lane >= d, pltpu.roll(x, d, axis=1), 0.0)
        d *= 2
    return x


def _mlstm_prep_kernel(xm_ref, xc_ref, bdq_ref, bdk_ref, bdv_ref, wif_ref, bif_ref,
                       q_ref, k_ref, v_ref, row_ref, col_ref, *, TP, T, n_pad):
    scale = M_HEAD_DIM ** -0.5
    for j in range(M_INNER // BD_TILE):
        sl = slice(j * BD_TILE, (j + 1) * BD_TILE)
        q_ref[:, sl] = jnp.dot(xc_ref[:, sl], bdq_ref[j], preferred_element_type=F32).astype(BF16)
        kj = jnp.dot(xc_ref[:, sl], bdk_ref[j], preferred_element_type=F32)
        k_ref[:, sl] = (kj * scale).astype(BF16)
        v_ref[:, sl] = jnp.dot(xm_ref[:, sl], bdv_ref[j], preferred_element_type=F32).astype(BF16)
    xcm = jnp.concatenate([xc_ref[...], xm_ref[...]], axis=1)
    g_all = jnp.dot(xcm, wif_ref[...], preferred_element_type=F32)[:, 0:LANES] + bif_ref[...]
    lane = lax.broadcasted_iota(jnp.int32, (T, LANES), 1)
    subl = lax.broadcasted_iota(jnp.int32, (SUBLANES, T), 0)
    for c in range(TP // T):
        g = g_all[c * T:(c + 1) * T, :]
        if n_pad:
            rowv = lax.broadcasted_iota(jnp.int32, (T, LANES), 0) >= n_pad
            g = jnp.where(jnp.logical_and(lane < M_HEADS, jnp.logical_not(rowv)), NEG, g)
        gt = g.T
        lf_t = jnp.minimum(gt, 0.0) - jnp.log1p(jnp.exp(-jnp.abs(gt)))
        cum_t = _lane_cumsum(lf_t, T)
        row_ref[c] = jnp.where(subl < M_HEADS, gt[0:SUBLANES, :], cum_t[0:SUBLANES, :])
        col_ref[c * T:(c + 1) * T, :] = jnp.where(lane < M_HEADS, g, cum_t.T)


def _mlstm_prep(xm, xc, weights, TP, T, n_pad):
    nb, L, _ = xm.shape
    blk = pl.BlockSpec((None, TP, M_INNER), lambda b, i: (b, i, 0))
    big = jax.ShapeDtypeStruct((nb, L, M_INNER), BF16)
    return pl.pallas_call(
        functools.partial(_mlstm_prep_kernel, TP=TP, T=T, n_pad=n_pad),
        grid=(nb, L // TP),
        in_specs=[blk, blk, *[_const_spec(w.shape) for w in weights]],
        out_specs=[blk, blk, blk,
                   pl.BlockSpec((None, TP // T, SUBLANES, T), lambda b, i: (b, i, 0, 0)),
                   pl.BlockSpec((None, TP, LANES), lambda b, i: (b, i, 0))],
        out_shape=[big, big, big,
                   jax.ShapeDtypeStruct((nb, L // T, SUBLANES, T), F32),
                   jax.ShapeDtypeStruct((nb, L, LANES), F32)],
        compiler_params=pltpu.CompilerParams(
            dimension_semantics=("arbitrary", "arbitrary"),
            vmem_limit_bytes=VMEM_LIMIT_BYTES),
        name="mlstm_prep",
    )(xm, xc, *weights)


def _mlstm_kernel(q_ref, k_ref, v_ref, xc_ref, zs_ref, row_ref, col_ref, skip_ref, gn_ref,
                  c0_ref, n0_ref, m0_ref, h_ref, *rest, T, nsub, emit_state):
    if emit_state:
        co_ref, no_ref, mo_ref, c_s, cb_s, n_s, m_s = rest
    else:
        c_s, cb_s, n_s, m_s = rest
    ci = pl.program_id(1)
    nc = pl.num_programs(1)

    @pl.when(ci == 0)
    def _():
        c_s[...] = c0_ref[...]
        cb_s[...] = c0_ref[...].astype(BF16)
        n_s[...] = n0_ref[...]
        m_s[...] = m0_ref[...]

    causal = (lax.broadcasted_iota(jnp.int32, (T, T), 0)
              >= lax.broadcasted_iota(jnp.int32, (T, T), 1))
    heads = range(M_HEADS)
    sls = [slice(h * M_HEAD_DIM, (h + 1) * M_HEAD_DIM) for h in heads]

    def chunk(rows, rowp):
        colp = col_ref[rows, :]

        s_raw = [lax.dot_general(q_ref[rows, sls[h]], k_ref[rows, sls[h]], _NT,
                                 preferred_element_type=F32) for h in heads]
        qc = [jnp.dot(q_ref[rows, sls[h]], cb_s[h], preferred_element_type=F32) for h in heads]

        dmat, w_inter, e_negm, w_k, decay, m_new = [], [], [], [], [], []
        for h in heads:
            i_row = rowp[h:h + 1, :]
            f_row = rowp[M_HEADS + h:M_HEADS + h + 1, :]
            i_col = colp[:, h:h + 1]
            f_col = colp[:, M_HEADS + h:M_HEADS + h + 1]
            g_tot = f_row[:, T - 1:T]
            m_prev = m_s[h][0:1, 0:1]
            b_row = i_row - f_row
            log_d = jnp.where(causal, f_col + b_row, NEG)
            inter = f_col + m_prev
            m_row = jnp.maximum(inter, jnp.max(log_d, axis=1, keepdims=True))
            dmat.append(jnp.exp(log_d - m_row))
            w_inter.append(jnp.exp(inter - m_row))
            e_negm.append(jnp.exp(-m_row))
            mn = jnp.maximum(g_tot + m_prev, jnp.max(g_tot + b_row, axis=1, keepdims=True))
            m_new.append(mn)
            w_k.append(jnp.exp(g_tot - f_col + i_col - mn))
            decay.append(jnp.exp(g_tot + m_prev - mn))

        hhs = []
        for h in heads:
            qn = jnp.sum(q_ref[rows, sls[h]].astype(F32) * n_s[h][0:1, :], axis=1, keepdims=True)
            s = s_raw[h] * dmat[h]
            num = (jnp.dot(s.astype(BF16), v_ref[rows, sls[h]], preferred_element_type=F32)
                   + w_inter[h] * qc[h])
            den = jnp.sum(s, axis=1, keepdims=True) + w_inter[h] * qn
            hhs.append(num * (1.0 / jnp.maximum(jnp.abs(den), e_negm[h])))
        for h in heads:
            kw = k_ref[rows, sls[h]].astype(F32) * w_k[h]
            c_new = decay[h] * c_s[h] + lax.dot_general(kw.astype(BF16), v_ref[rows, sls[h]], _TN,
                                                        preferred_element_type=F32)
            n_new = decay[h] * n_s[h][0:1, :] + jnp.sum(kw, axis=0, keepdims=True)
            c_s[h] = c_new
            cb_s[h] = c_new.astype(BF16)
            n_s[h] = jnp.broadcast_to(n_new, (SUBLANES, M_HEAD_DIM))
            m_s[h] = jnp.broadcast_to(m_new[h], (SUBLANES, LANES))
        for h in heads:
            hn = _head_norm(hhs[h], gn_ref[:, sls[h]])
            out = ((hn + skip_ref[:, sls[h]] * xc_ref[rows, sls[h]].astype(F32))
                   * zs_ref[rows, sls[h]].astype(F32))
            h_ref[rows, sls[h]] = out.astype(BF16)

    for cc in range(nsub):
        chunk(slice(cc * T, (cc + 1) * T), row_ref[cc])

    if emit_state:
        @pl.when(ci == nc - 1)
        def _():
            co_ref[...] = c_s[...]
            no_ref[...] = n_s[...]
            mo_ref[...] = m_s[...]


def _mlstm(q, k, v, xc, zs, rowp, colp, skip, gn, state, T, emit_state):
    nb, L, _ = q.shape
    nsub = min(MLSTM_STEP_CHUNKS, L // T)
    nc = L // (T * nsub)
    blk = pl.BlockSpec((None, T * nsub, M_INNER), lambda b, c: (b, c, 0))
    in_specs = [blk, blk, blk, blk, blk,
                pl.BlockSpec((None, nsub, SUBLANES, T), lambda b, c: (b, c, 0, 0)),
                pl.BlockSpec((None, T * nsub, LANES), lambda b, c: (b, c, 0)),
                _const_spec(skip.shape), _const_spec(gn.shape),
                *[_const_spec(s.shape) for s in state]]
    out_shape = [jax.ShapeDtypeStruct((nb, L, M_INNER), BF16)]
    out_specs = [blk]
    if emit_state:
        assert nb == 1
        for s in state:
            out_shape.append(jax.ShapeDtypeStruct(s.shape, F32))
            out_specs.append(pl.BlockSpec(s.shape, lambda b, c, _n=len(s.shape): (0,) * _n))
    c_shape = state[0].shape
    scratch = [pltpu.VMEM(c_shape, F32), pltpu.VMEM(c_shape, BF16),
               pltpu.VMEM(state[1].shape, F32), pltpu.VMEM(state[2].shape, F32)]
    return pl.pallas_call(
        functools.partial(_mlstm_kernel, T=T, nsub=nsub, emit_state=emit_state),
        grid=(nb, nc),
        in_specs=in_specs,
        out_specs=out_specs,
        out_shape=out_shape,
        scratch_shapes=scratch,
        compiler_params=pltpu.CompilerParams(
            dimension_semantics=("arbitrary", "arbitrary"),
            vmem_limit_bytes=MLSTM_VMEM_LIMIT_BYTES),
        name="mlstm_state" if emit_state else "mlstm",
    )(q, k, v, xc, zs, rowp, colp, skip, gn, *state)


def _retention_kernel(q_ref, k_ref, v_ref, gs_ref, gn_ref, s0_ref,
                      o_ref, *rest, T, nsub, emit_state):
    if emit_state:
        so_ref, st_s, dm_s = rest
    else:
        st_s, dm_s = rest
    ci = pl.program_id(1)
    nc = pl.num_programs(1)
    log_gamma = [math.log(1.0 - 2.0 ** (-5.0 - h)) for h in range(R_HEADS)]

    @pl.when(ci == 0)
    def _():
        st_s[...] = s0_ref[...]
        diff = (lax.broadcasted_iota(jnp.int32, (T, T), 0)
                - lax.broadcasted_iota(jnp.int32, (T, T), 1))
        dfl = diff.astype(F32)
        for h in range(R_HEADS):
            dm_s[h] = jnp.where(diff >= 0, jnp.exp(log_gamma[h] * dfl), 0.0)

    tf = lax.broadcasted_iota(jnp.int32, (T, 1), 0).astype(F32)
    for cc in range(nsub):
        rows = slice(cc * T, (cc + 1) * T)
        for h in range(R_HEADS):
            qh = q_ref[rows, h * R_QK_DIM:(h + 1) * R_QK_DIM]
            kh = k_ref[rows, h * R_QK_DIM:(h + 1) * R_QK_DIM]
            sl = slice(h * R_V_DIM, (h + 1) * R_V_DIM)
            vh = v_ref[rows, sl]
            s = lax.dot_general(qh, kh, _NT, preferred_element_type=F32) * dm_s[h]
            q_decay = jnp.exp(log_gamma[h] * (tf + 1.0))
            k_decay = jnp.exp(log_gamma[h] * (T - 1.0 - tf))
            o = (jnp.dot(s.astype(BF16), vh, preferred_element_type=F32)
                 + q_decay * jnp.dot(qh, st_s[h].astype(BF16), preferred_element_type=F32))
            kd = (kh.astype(F32) * k_decay).astype(BF16)
            st_s[h] = (math.exp(log_gamma[h] * T) * st_s[h]
                       + lax.dot_general(kd, vh, _TN, preferred_element_type=F32))
            o_ref[rows, sl] = (_head_norm(o, gn_ref[:, sl])
                               * gs_ref[rows, sl].astype(F32)).astype(BF16)

    if emit_state:
        @pl.when(ci == nc - 1)
        def _():
            so_ref[...] = st_s[...]


def _retention(q, k, v, gs, gn, s0, T, emit_state):
    nb, L, _ = q.shape
    nsub = min(RET_STEP_CHUNKS, L // T)
    nc = L // (T * nsub)
    qk_blk = pl.BlockSpec((None, T * nsub, R_QK), lambda b, c: (b, c, 0))
    v_blk = pl.BlockSpec((None, T * nsub, R_V), lambda b, c: (b, c, 0))
    in_specs = [qk_blk, qk_blk, v_blk, v_blk, _const_spec(gn.shape), _const_spec(s0.shape)]
    out_shape = [jax.ShapeDtypeStruct((nb, L, R_V), BF16)]
    out_specs = [v_blk]
    if emit_state:
        assert nb == 1
        out_shape.append(jax.ShapeDtypeStruct(s0.shape, F32))
        out_specs.append(pl.BlockSpec(s0.shape, lambda b, c: (0, 0, 0)))
    scratch = [pltpu.VMEM(s0.shape, F32), pltpu.VMEM((R_HEADS, T, T), F32)]
    return pl.pallas_call(
        functools.partial(_retention_kernel, T=T, nsub=nsub, emit_state=emit_state),
        grid=(nb, nc),
        in_specs=in_specs,
        out_specs=out_specs,
        out_shape=out_shape,
        scratch_shapes=scratch,
        compiler_params=pltpu.CompilerParams(
            dimension_semantics=("arbitrary", "arbitrary"),
            vmem_limit_bytes=VMEM_LIMIT_BYTES),
        name="retention_state" if emit_state else "retention",
    )(q, k, v, gs, gn, s0)


def _rms(x, g):
    return x * lax.rsqrt(jnp.mean(x * x, axis=-1, keepdims=True) + EPS) * g


def _out_ffn_kernel(hm_ref, hr_ref, ga_ref, gb_ref, x_ref, wpm_ref, wpr_ref, wout_ref,
                    wup_ref, wdn_ref, g1_ref, g2_ref, g3_ref, o_ref):
    ya = jnp.dot(hm_ref[...], wpm_ref[...], preferred_element_type=F32)
    yb = jnp.dot(hr_ref[...], wpr_ref[...], preferred_element_type=F32)
    mixin = ga_ref[...].astype(F32) * ya + gb_ref[...].astype(F32) * yb
    mix = jnp.dot(mixin.astype(BF16), wout_ref[...], preferred_element_type=F32)
    h1 = x_ref[...] + _rms(mix, g1_ref[...])
    u = _rms(h1, g2_ref[...]).astype(BF16)
    f = None
    for j in range(D_FF // FF_TILE):
        sl = slice(j * FF_TILE, (j + 1) * FF_TILE)
        a = jnp.maximum(jnp.dot(u, wup_ref[:, sl], preferred_element_type=F32), 0.0)
        part = jnp.dot((a * a).astype(BF16), wdn_ref[sl, :], preferred_element_type=F32)
        f = part if f is None else f + part
    o_ref[...] = h1 + _rms(f, g3_ref[...])


def _out_ffn(hm, hr, ga, gb, x2d, wpm, wpr, wout, wup, wdn, g1, g2, g3, tm):
    rows = x2d.shape[0]
    row_spec = lambda w: pl.BlockSpec((tm, w), lambda i: (i, 0))
    consts = (wpm, wpr, wout, wup, wdn, g1, g2, g3)
    return pl.pallas_call(
        _out_ffn_kernel,
        grid=(rows // tm,),
        in_specs=[row_spec(M_INNER), row_spec(R_V), row_spec(D_MODEL), row_spec(D_MODEL),
                  row_spec(D_MODEL), *[_const_spec(c.shape) for c in consts]],
        out_specs=row_spec(D_MODEL),
        out_shape=jax.ShapeDtypeStruct((rows, D_MODEL), F32),
        compiler_params=pltpu.CompilerParams(
            dimension_semantics=("arbitrary",),
            vmem_limit_bytes=VMEM_LIMIT_BYTES),
        name="out_ffn",
    )(hm, hr, ga, gb, x2d, *consts)


def _block_diag_tiles(w):
    rows = w.reshape(M_INNER, M_QKV_BLOCK)
    wide = jnp.tile(rows, (1, BD_TILE // M_QKV_BLOCK))
    r_blk = (lax.broadcasted_iota(jnp.int32, wide.shape, 0) % BD_TILE) // M_QKV_BLOCK
    c_blk = lax.broadcasted_iota(jnp.int32, wide.shape, 1) // M_QKV_BLOCK
    t = jnp.where(r_blk == c_blk, wide, 0.0).astype(BF16)
    return t.reshape(M_INNER // BD_TILE, BD_TILE, BD_TILE)


def _rope_tables(pos0, n):
    half = R_QK_DIM // 2
    inv = ROPE_BASE ** (-np.arange(half, dtype=np.float64) / half)
    ang = (pos0 + np.arange(n, dtype=np.float64))[:, None] * inv[None, :]
    return jnp.asarray(np.cos(ang), F32), jnp.asarray(np.sin(ang), F32)


def kernel(x, meta_tokens, norm_mix_pre, w_in, conv_w, conv_b, w_q_m, w_k_m, w_v_m, w_if, b_if,
           skip_m, gn_m, gn_r, w_proj_m, w_proj_r, w_out, norm_mix_post, norm_ffn_pre, w_up,
           w_down, norm_ffn_post):
    nb, seq, _ = x.shape
    assert norm_mix_pre.shape[0] == 1, "single-layer block"
    assert seq % PROJ_TM == 0 and seq % PREP_TP == 0 and seq % (RET_STEP_CHUNKS * CHUNK_T) == 0
    row = lambda a: a[0].reshape(1, -1).astype(F32)
    n_pad = META_T - N_META

    w_in_bf = w_in[0].astype(BF16)
    g_pre = row(norm_mix_pre)
    cw, cb = conv_w[0].astype(F32), row(conv_b)

    meta_rows = jnp.concatenate(
        [jnp.zeros((n_pad, D_MODEL), x.dtype), meta_tokens.astype(x.dtype)], axis=0)
    cos_m, sin_m = _rope_tables(-n_pad, META_T)
    cos_x, sin_x = _rope_tables(N_META, seq)
    tail_zero = jnp.zeros((SUBLANES, M_INNER), F32)
    m_meta, r_meta, _ = _in_proj_all(meta_rows, g_pre, w_in_bf, cw, cb, tail_zero, cos_m, sin_m,
                                     META_T, 1, n_pad)
    x2d = x.reshape(nb * seq, D_MODEL)
    m_in, r_in, (ga, gb) = _in_proj_all(x2d, g_pre, w_in_bf, cw, cb, m_meta[3], cos_x, sin_x,
                                        PROJ_TM, seq // PROJ_TM, 0)

    nblk = M_INNER // M_QKV_BLOCK
    wif3 = w_if[0].reshape(3, nblk, M_QKV_BLOCK, 2 * M_HEADS)
    fold = lambda w, part: jnp.einsum('nio,noj->nij', w, part).reshape(M_INNER, 2 * M_HEADS)
    wg = jnp.concatenate([fold(w_q_m[0], wif3[0]) + fold(w_k_m[0], wif3[1]), fold(w_v_m[0], wif3[2])],
                         axis=0)
    wif = jnp.zeros((2 * M_INNER, GATE_PAD), F32).at[:, :2 * M_HEADS].set(wg).astype(BF16)
    bif = jnp.zeros((1, LANES), F32).at[0, :2 * M_HEADS].set(b_if[0])
    p_weights = (_block_diag_tiles(w_q_m[0]), _block_diag_tiles(w_k_m[0]),
                 _block_diag_tiles(w_v_m[0]), wif, bif)
    m_zero = (jnp.zeros((M_HEADS, M_HEAD_DIM, M_HEAD_DIM), F32),
              jnp.zeros((M_HEADS, SUBLANES, M_HEAD_DIM), F32),
              jnp.zeros((M_HEADS, SUBLANES, LANES), F32))
    as_seq = lambda a, n, t: a.reshape(n, t, a.shape[-1])
    xm_m, xc_m, zs_m = [as_seq(a, 1, META_T) for a in m_meta[:3]]
    xm_x, xc_x, zs_x = [as_seq(a, nb, seq) for a in m_in[:3]]
    prep_m = _mlstm_prep(xm_m, xc_m, p_weights, META_T, META_T, n_pad)
    m_init = _mlstm(*prep_m[:3], xc_m, zs_m, *prep_m[3:], row(skip_m), row(gn_m), m_zero,
                    META_T, True)[1:]
    prep_x = _mlstm_prep(xm_x, xc_x, p_weights, PREP_TP, CHUNK_T, 0)
    hm = _mlstm(*prep_x[:3], xc_x, zs_x, *prep_x[3:], row(skip_m), row(gn_m), tuple(m_init),
                CHUNK_T, False)[0]

    r_zero = jnp.zeros((R_HEADS, R_QK_DIM, R_V_DIM), F32)
    r_init = _retention(*[as_seq(a, 1, META_T) for a in r_meta], row(gn_r), r_zero,
                        META_T, True)[1]
    hr = _retention(*[as_seq(a, nb, seq) for a in r_in], row(gn_r), r_init, CHUNK_T, False)[0]

    out = _out_ffn(hm.reshape(nb * seq, M_INNER), hr.reshape(nb * seq, R_V), ga, gb, x2d,
                   w_proj_m[0].astype(BF16), w_proj_r[0].astype(BF16), w_out[0].astype(BF16),
                   w_up[0].astype(BF16), w_down[0].astype(BF16),
                   row(norm_mix_post), row(norm_ffn_pre), row(norm_ffn_post), OUT_TM)
    return out.reshape(nb, seq, D_MODEL)
```

```python
import functools
import math

import jax
import jax.numpy as jnp
import numpy as np
from jax import lax
from jax.experimental import pallas as pl
from jax.experimental.pallas import tpu as pltpu

F32 = jnp.float32
BF16 = jnp.bfloat16

D_MODEL = 1024
N_META = 16
M_INNER = 2 * D_MODEL
M_HEADS = 4
M_HEAD_DIM = M_INNER // M_HEADS
M_QKV_BLOCK = 4
M_CONV = 4
R_HEADS = 4
R_QK_DIM = D_MODEL // R_HEADS
R_V_DIM = 2 * R_QK_DIM
R_QK = R_HEADS * R_QK_DIM
R_V = R_HEADS * R_V_DIM
D_FF = 4 * D_MODEL
ROPE_BASE = 10000.0
EPS = 1e-6
NEG = -1e30
LOG2E = math.log2(math.e)
N_IN = 2 * M_INNER + 2 * R_QK + 2 * R_V + 2 * D_MODEL

LANES = 128
SUBLANES = 8
MXU_DIM = 256
VMEM_LIMIT_BYTES = 56 * 1024 * 1024

CHUNK_T = 256
META_T = 128
BD_TILE = MXU_DIM
GATE_PAD = MXU_DIM
MLSTM_STEP_CHUNKS = 1
RET_STEP_CHUNKS = 4
PREP_TP = 1024
CONV_STRIP = 512
PROJ_TM = 1024
PROJ_TM_CONV = 512
OUT_TM = 512
FF_TILE = 1024

_NT = (((1,), (1,)), ((), ()))
_TN = (((0,), (0,)), ((), ()))


def _const_spec(shape):
    nd = len(shape)
    return pl.BlockSpec(shape, lambda *_: (0,) * nd, pipeline_mode=pl.Buffered(1))


def _sigmoid(x):
    return 1.0 / (1.0 + jnp.exp2(x * (-LOG2E)))


def _silu(x):
    return x * _sigmoid(x)


def _normed(x_ref):
    x = x_ref[...]
    ms = jnp.mean(x * x, axis=-1, keepdims=True)
    return (x * lax.rsqrt(ms + EPS)).astype(BF16)


def _inproj_mlstm_kernel(x_ref, w_ref, wv_ref, wg_ref, cw_ref, cb_ref, t0_ref,
                         xm_ref, xc_ref, zs_ref, v_ref, gs_ref, to_ref, halo_s, res_s,
                         *, tm, tiles_per_seq):
    i = pl.program_id(0)

    @pl.when(i % tiles_per_seq == 0)
    def _():
        halo_s[...] = t0_ref[...]

    u = _normed(x_ref)
    nstrip = M_INNER // CONV_STRIP
    nv = tm // SUBLANES
    sub = lax.broadcasted_iota(jnp.int32, (nv, SUBLANES, CONV_STRIP), 1)

    def strip_dot(c):
        sl = slice(c * CONV_STRIP, (c + 1) * CONV_STRIP)
        res_s[(i + c) % 2] = jnp.dot(u, w_ref[:, sl], preferred_element_type=F32)

    strip_dot(0)
    for c in range(nstrip):
        sl = slice(c * CONV_STRIP, (c + 1) * CONV_STRIP)
        if c + 1 < nstrip:
            strip_dot(c + 1)
        zsl = slice(M_INNER + c * CONV_STRIP, M_INNER + (c + 1) * CONV_STRIP)
        zm = jnp.dot(u, w_ref[:, zsl], preferred_element_type=F32)
        zs_ref[:, sl] = _silu(zm).astype(BF16)
        v_ref[:, sl] = jnp.dot(u, wv_ref[:, sl], preferred_element_type=F32).astype(BF16)
        gs_ref[:, sl] = _silu(jnp.dot(u, wg_ref[:, sl], preferred_element_type=F32)).astype(BF16)
        xm = res_s[(i + c) % 2]
        xm_ref[:, sl] = xm.astype(BF16)
        x3 = xm.reshape(nv, SUBLANES, CONV_STRIP)
        prev3 = jnp.concatenate([halo_s[:, sl].reshape(1, SUBLANES, CONV_STRIP), x3[:nv - 1]], axis=0)
        y = cb_ref[:, sl] + cw_ref[M_CONV - 1:M_CONV, sl] * x3
        for j in range(M_CONV - 1):
            sh = M_CONV - 1 - j
            merged = jnp.where(sub >= SUBLANES - sh, prev3, x3)
            y = y + cw_ref[j:j + 1, sl] * pltpu.roll(merged, sh, axis=1)
        halo_s[:, sl] = xm[tm - SUBLANES:, :]
        xc_ref[:, sl] = _silu(y.reshape(tm, CONV_STRIP)).astype(BF16)

    @pl.when(i == pl.num_programs(0) - 1)
    def _():
        to_ref[...] = halo_s[...]


def _inproj_rot_gate_kernel(x_ref, w_ref, wmix_ref, cos_ref, sin_ref,
                            q_ref, k_ref, ga_ref, gb_ref, *, tm, n_pad):
    u = _normed(x_ref)
    cos = cos_ref[...]
    sin = sin_ref[...]
    half = R_QK_DIM // 2
    kscale = R_QK_DIM ** -0.5
    if n_pad:
        kmul = (lax.broadcasted_iota(jnp.int32, (tm, 1), 0) >= n_pad).astype(F32) * kscale

    def rot(t, h):
        t1 = t[:, h * R_QK_DIM:h * R_QK_DIM + half]
        t2 = t[:, h * R_QK_DIM + half:(h + 1) * R_QK_DIM]
        return t1 * cos - t2 * sin, t1 * sin + t2 * cos

    q = jnp.dot(u, w_ref[:, 0:R_QK], preferred_element_type=F32)
    for h in range(R_HEADS):
        a, b = rot(q, h)
        q_ref[:, h * R_QK_DIM:h * R_QK_DIM + half] = a.astype(BF16)
        q_ref[:, h * R_QK_DIM + half:(h + 1) * R_QK_DIM] = b.astype(BF16)
    k = jnp.dot(u, w_ref[:, R_QK:2 * R_QK], preferred_element_type=F32)
    for h in range(R_HEADS):
        a, b = rot(k, h)
        if n_pad:
            a, b = a * kmul, b * kmul
        else:
            a, b = a * kscale, b * kscale
        k_ref[:, h * R_QK_DIM:h * R_QK_DIM + half] = a.astype(BF16)
        k_ref[:, h * R_QK_DIM + half:(h + 1) * R_QK_DIM] = b.astype(BF16)
    ga_ref[...] = _sigmoid(jnp.dot(u, wmix_ref[:, 0:D_MODEL], preferred_element_type=F32)).astype(BF16)
    gb_ref[...] = _sigmoid(
        jnp.dot(u, wmix_ref[:, D_MODEL:2 * D_MODEL], preferred_element_type=F32)).astype(BF16)


def _w_cols(width, start):
    assert start % width == 0
    return pl.BlockSpec((D_MODEL, width), lambda *_: (0, start // width), pipeline_mode=pl.Buffered(1))


def _in_proj_call(body, name, x2d, w_bf, w_specs, extra_in, extra_specs, out_widths, tm,
                  extra_out=(), extra_out_specs=(), scratch=()):
    rows = x2d.shape[0]
    row_spec = lambda w: pl.BlockSpec((tm, w), lambda i: (i, 0))
    return pl.pallas_call(
        body,
        grid=(rows // tm,),
        in_specs=[row_spec(D_MODEL), *w_specs, *extra_specs],
        out_specs=[*[row_spec(w) for w in out_widths], *extra_out_specs],
        out_shape=[*[jax.ShapeDtypeStruct((rows, w), BF16) for w in out_widths], *extra_out],
        scratch_shapes=list(scratch),
        compiler_params=pltpu.CompilerParams(
            dimension_semantics=("arbitrary",),
            vmem_limit_bytes=VMEM_LIMIT_BYTES),
        name=name,
    )(x2d, *([w_bf] * len(w_specs)), *extra_in)


def _in_proj_all(x2d, w_bf, cw, cb, tail0, cos, sin, tm, tiles_per_seq, n_pad):
    c_qk = 2 * M_INNER
    c_v = c_qk + 2 * R_QK
    c_g = c_v + R_V
    c_mix = c_g + R_V
    tail_shape = (SUBLANES, M_INNER)
    tmc = min(tm, PROJ_TM_CONV)
    xm, xc, zs, v, gs, tail = _in_proj_call(
        functools.partial(_inproj_mlstm_kernel, tm=tmc, tiles_per_seq=tiles_per_seq * tm // tmc),
        "in_proj_mlstm", x2d, w_bf,
        [_w_cols(2 * M_INNER, 0), _w_cols(R_V, c_v), _w_cols(R_V, c_g)], (cw, cb, tail0),
        (_const_spec(cw.shape), _const_spec(cb.shape), _const_spec(tail_shape)),
        (M_INNER, M_INNER, M_INNER, R_V, R_V), tmc,
        extra_out=(jax.ShapeDtypeStruct(tail_shape, F32),),
        extra_out_specs=(pl.BlockSpec(tail_shape, lambda i: (0, 0)),),
        scratch=(pltpu.VMEM(tail_shape, F32), pltpu.VMEM((2, tmc, CONV_STRIP), F32)))
    pos_spec = pl.BlockSpec((tm, R_QK_DIM // 2), lambda i: (i % tiles_per_seq, 0))
    qr, kr, ga, gb = _in_proj_call(
        functools.partial(_inproj_rot_gate_kernel, tm=tm, n_pad=n_pad),
        "in_proj_rot_gate", x2d, w_bf, [_w_cols(2 * R_QK, c_qk), _w_cols(2 * D_MODEL, c_mix)],
        (cos, sin), (pos_spec, pos_spec), (R_QK, R_QK, D_MODEL, D_MODEL), tm)
    return (xm, xc, zs, tail), (qr, kr, v, gs), (ga, gb)


def _head_norm(h, w):
    mu = jnp.mean(h, axis=-1, keepdims=True)
    c = h - mu
    var = jnp.mean(c * c, axis=-1, keepdims=True)
    return c * lax.rsqrt(var + EPS) * w


def _lane_cumsum(x, n):
    lane = lax.broadcasted_iota(jnp.int32, x.shape, 1)
    d = 1
    while d < n:
        x = x + jnp.where(lane >= d, pltpu.roll(x, d, axis=1), 0.0)
        d *= 2
    return x


def _mlstm_prep_kernel(xm_ref, xc_ref, bdq_ref, bdk_ref, bdv_ref, wif_ref, bif_ref,
                       q_ref, k_ref, v_ref, row_ref, col_ref, *, TP, T, n_pad):
    scale = M_HEAD_DIM ** -0.5
    for j in range(M_INNER // BD_TILE):
        sl = slice(j * BD_TILE, (j + 1) * BD_TILE)
        q_ref[:, sl] = jnp.dot(xc_ref[:, sl], bdq_ref[j], preferred_element_type=F32).astype(BF16)
        kj = jnp.dot(xc_ref[:, sl], bdk_ref[j], preferred_element_type=F32)
        k_ref[:, sl] = (kj * scale).astype(BF16)
        v_ref[:, sl] = jnp.dot(xm_ref[:, sl], bdv_ref[j], preferred_element_type=F32).astype(BF16)
    xcm = jnp.concatenate([xc_ref[...], xm_ref[...]], axis=1)
    g_all = jnp.dot(xcm, wif_ref[...], preferred_element_type=F32)[:, 0:LANES] + bif_ref[...]
    lane = lax.broadcasted_iota(jnp.int32, (T, LANES), 1)
    subl = lax.broadcasted_iota(jnp.int32, (SUBLANES, T), 0)
    for c in range(TP // T):
        g = g_all[c * T:(c + 1) * T, :]
        if n_pad:
            rowv = lax.broadcasted_iota(jnp.int32, (T, LANES), 0) >= n_pad
            g = jnp.where(jnp.logical_and(lane < M_HEADS, jnp.logical_not(rowv)), NEG, g)
        gt = g.T
        lf_t = jnp.minimum(gt, 0.0) - jnp.log1p(jnp.exp(-jnp.abs(gt)))
        cum_t = _lane_cumsum(lf_t, T)
        row_ref[c] = jnp.where(subl < M_HEADS, gt[0:SUBLANES, :], cum_t[0:SUBLANES, :])
        col_ref[c * T:(c + 1) * T, :] = jnp.where(lane < M_HEADS, g, cum_t.T)


def _mlstm_prep(xm, xc, weights, TP, T, n_pad):
    nb, L, _ = xm.shape
    blk = pl.BlockSpec((None, TP, M_INNER), lambda b, i: (b, i, 0))
    big = jax.ShapeDtypeStruct((nb, L, M_INNER), BF16)
    return pl.pallas_call(
        functools.partial(_mlstm_prep_kernel, TP=TP, T=T, n_pad=n_pad),
        grid=(nb, L // TP),
        in_specs=[blk, blk, *[_const_spec(w.shape) for w in weights]],
        out_specs=[blk, blk, blk,
                   pl.BlockSpec((None, TP // T, SUBLANES, T), lambda b, i: (b, i, 0, 0)),
                   pl.BlockSpec((None, TP, LANES), lambda b, i: (b, i, 0))],
        out_shape=[big, big, big,
                   jax.ShapeDtypeStruct((nb, L // T, SUBLANES, T), F32),
                   jax.ShapeDtypeStruct((nb, L, LANES), F32)],
        compiler_params=pltpu.CompilerParams(
            dimension_semantics=("arbitrary", "arbitrary"),
            vmem_limit_bytes=VMEM_LIMIT_BYTES),
        name="mlstm_prep",
    )(xm, xc, *weights)


def _mlstm_kernel(q_ref, k_ref, v_ref, xc_ref, zs_ref, row_ref, col_ref, skip_ref, gn_ref,
                  c0_ref, n0_ref, m0_ref, h_ref, *rest, T, nsub, emit_state):
    if emit_state:
        co_ref, no_ref, mo_ref, c_s, cb_s, n_s, m_s = rest
    else:
        c_s, cb_s, n_s, m_s = rest
    ci = pl.program_id(1)
    nc = pl.num_programs(1)

    @pl.when(ci == 0)
    def _():
        c_s[...] = c0_ref[...]
        cb_s[...] = c0_ref[...].astype(BF16)
        n_s[...] = n0_ref[...]
        m_s[...] = m0_ref[...]

    causal = (lax.broadcasted_iota(jnp.int32, (T, T), 0)
              >= lax.broadcasted_iota(jnp.int32, (T, T), 1))
    heads = range(M_HEADS)
    sls = [slice(h * M_HEAD_DIM, (h + 1) * M_HEAD_DIM) for h in heads]

    def chunk(rows, rowp):
        colp = col_ref[rows, :]

        s_raw = [lax.dot_general(q_ref[rows, sls[h]], k_ref[rows, sls[h]], _NT,
                                 preferred_element_type=F32) for h in heads]
        qc = [jnp.dot(q_ref[rows, sls[h]], cb_s[h], preferred_element_type=F32) for h in heads]

        dmat, w_inter, e_negm, w_k, decay, m_new = [], [], [], [], [], []
        for h in heads:
            i_row = rowp[h:h + 1, :]
            f_row = rowp[M_HEADS + h:M_HEADS + h + 1, :]
            i_col = colp[:, h:h + 1]
            f_col = colp[:, M_HEADS + h:M_HEADS + h + 1]
            g_tot = f_row[:, T - 1:T]
            m_prev = m_s[h][0:1, 0:1]
            b_row = i_row - f_row
            log_d = jnp.where(causal, f_col + b_row, NEG)
            inter = f_col + m_prev
            m_row = jnp.maximum(inter, jnp.max(log_d, axis=1, keepdims=True))
            dmat.append(jnp.exp(log_d - m_row))
            w_inter.append(jnp.exp(inter - m_row))
            e_negm.append(jnp.exp(-m_row))
            mn = jnp.maximum(g_tot + m_prev, jnp.max(g_tot + b_row, axis=1, keepdims=True))
            m_new.append(mn)
            w_k.append(jnp.exp(g_tot - f_col + i_col - mn))
            decay.append(jnp.exp(g_tot + m_prev - mn))

        hhs = []
        for h in heads:
            qn = jnp.sum(q_ref[rows, sls[h]].astype(F32) * n_s[h][0:1, :], axis=1, keepdims=True)
            s = s_raw[h] * dmat[h]
            num = (jnp.dot(s.astype(BF16), v_ref[rows, sls[h]], preferred_element_type=F32)
                   + w_inter[h] * qc[h])
            den = jnp.sum(s, axis=1, keepdims=True) + w_inter[h] * qn
            hhs.append(num * (1.0 / jnp.maximum(jnp.abs(den), e_negm[h])))
        for h in heads:
            kw = k_ref[rows, sls[h]].astype(F32) * w_k[h]
            c_new = decay[h] * c_s[h] + lax.dot_general(kw.astype(BF16), v_ref[rows, sls[h]], _TN,
                                                        preferred_element_type=F32)
            n_new = decay[h] * n_s[h][0:1, :] + jnp.sum(kw, axis=0, keepdims=True)
            c_s[h] = c_new
            cb_s[h] = c_new.astype(BF16)
            n_s[h] = jnp.broadcast_to(n_new, (SUBLANES, M_HEAD_DIM))
            m_s[h] = jnp.broadcast_to(m_new[h], (SUBLANES, LANES))
        for h in heads:
            hn = _head_norm(hhs[h], gn_ref[:, sls[h]])
            out = ((hn + skip_ref[:, sls[h]] * xc_ref[rows, sls[h]].astype(F32))
                   * zs_ref[rows, sls[h]].astype(F32))
            h_ref[rows, sls[h]] = out.astype(BF16)

    for cc in range(nsub):
        chunk(slice(cc * T, (cc + 1) * T), row_ref[cc])

    if emit_state:
        @pl.when(ci == nc - 1)
        def _():
            co_ref[...] = c_s[...]
            no_ref[...] = n_s[...]
            mo_ref[...] = m_s[...]


def _mlstm(q, k, v, xc, zs, rowp, colp, skip, gn, state, T, emit_state):
    nb, L, _ = q.shape
    nsub = min(MLSTM_STEP_CHUNKS, L // T)
    nc = L // (T * nsub)
    blk = pl.BlockSpec((None, T * nsub, M_INNER), lambda b, c: (b, c, 0))
    in_specs = [blk, blk, blk, blk, blk,
                pl.BlockSpec((None, nsub, SUBLANES, T), lambda b, c: (b, c, 0, 0)),
                pl.BlockSpec((None, T * nsub, LANES), lambda b, c: (b, c, 0)),
                _const_spec(skip.shape), _const_spec(gn.shape),
                *[_const_spec(s.shape) for s in state]]
    out_shape = [jax.ShapeDtypeStruct((nb, L, M_INNER), BF16)]
    out_specs = [blk]
    if emit_state:
        assert nb == 1
        for s in state:
            out_shape.append(jax.ShapeDtypeStruct(s.shape, F32))
            out_specs.append(pl.BlockSpec(s.shape, lambda b, c, _n=len(s.shape): (0,) * _n))
    c_shape = state[0].shape
    scratch = [pltpu.VMEM(c_shape, F32), pltpu.VMEM(c_shape, BF16),
               pltpu.VMEM(state[1].shape, F32), pltpu.VMEM(state[2].shape, F32)]
    return pl.pallas_call(
        functools.partial(_mlstm_kernel, T=T, nsub=nsub, emit_state=emit_state),
        grid=(nb, nc),
        in_specs=in_specs,
        out_specs=out_specs,
        out_shape=out_shape,
        scratch_shapes=scratch,
        compiler_params=pltpu.CompilerParams(
            dimension_semantics=("arbitrary", "arbitrary"),
            vmem_limit_bytes=VMEM_LIMIT_BYTES),
        name="mlstm_state" if emit_state else "mlstm",
    )(q, k, v, xc, zs, rowp, colp, skip, gn, *state)


def _retention_kernel(q_ref, k_ref, v_ref, gs_ref, gn_ref, s0_ref,
                      o_ref, *rest, T, nsub, emit_state):
    if emit_state:
        so_ref, st_s, dm_s = rest
    else:
        st_s, dm_s = rest
    ci = pl.program_id(1)
    nc = pl.num_programs(1)
    log_gamma = [math.log(1.0 - 2.0 ** (-5.0 - h)) for h in range(R_HEADS)]

    @pl.when(ci == 0)
    def _():
        st_s[...] = s0_ref[...]
        diff = (lax.broadcasted_iota(jnp.int32, (T, T), 0)
                - lax.broadcasted_iota(jnp.int32, (T, T), 1))
        dfl = diff.astype(F32)
        for h in range(R_HEADS):
            dm_s[h] = jnp.where(diff >= 0, jnp.exp(log_gamma[h] * dfl), 0.0)

    tf = lax.broadcasted_iota(jnp.int32, (T, 1), 0).astype(F32)
    for cc in range(nsub):
        rows = slice(cc * T, (cc + 1) * T)
        for h in range(R_HEADS):
            qh = q_ref[rows, h * R_QK_DIM:(h + 1) * R_QK_DIM]
            kh = k_ref[rows, h * R_QK_DIM:(h + 1) * R_QK_DIM]
            sl = slice(h * R_V_DIM, (h + 1) * R_V_DIM)
            vh = v_ref[rows, sl]
            s = lax.dot_general(qh, kh, _NT, preferred_element_type=F32) * dm_s[h]
            q_decay = jnp.exp(log_gamma[h] * (tf + 1.0))
            k_decay = jnp.exp(log_gamma[h] * (T - 1.0 - tf))
            o = (jnp.dot(s.astype(BF16), vh, preferred_element_type=F32)
                 + q_decay * jnp.dot(qh, st_s[h].astype(BF16), preferred_element_type=F32))
            kd = (kh.astype(F32) * k_decay).astype(BF16)
            st_s[h] = (math.exp(log_gamma[h] * T) * st_s[h]
                       + lax.dot_general(kd, vh, _TN, preferred_element_type=F32))
            o_ref[rows, sl] = (_head_norm(o, gn_ref[:, sl])
                               * gs_ref[rows, sl].astype(F32)).astype(BF16)

    if emit_state:
        @pl.when(ci == nc - 1)
        def _():
            so_ref[...] = st_s[...]


def _retention(q, k, v, gs, gn, s0, T, emit_state):
    nb, L, _ = q.shape
    nsub = min(RET_STEP_CHUNKS, L // T)
    nc = L // (T * nsub)
    qk_blk = pl.BlockSpec((None, T * nsub, R_QK), lambda b, c: (b, c, 0))
    v_blk = pl.BlockSpec((None, T * nsub, R_V), lambda b, c: (b, c, 0))
    in_specs = [qk_blk, qk_blk, v_blk, v_blk, _const_spec(gn.shape), _const_spec(s0.shape)]
    out_shape = [jax.ShapeDtypeStruct((nb, L, R_V), BF16)]
    out_specs = [v_blk]
    if emit_state:
        assert nb == 1
        out_shape.append(jax.ShapeDtypeStruct(s0.shape, F32))
        out_specs.append(pl.BlockSpec(s0.shape, lambda b, c: (0, 0, 0)))
    scratch = [pltpu.VMEM(s0.shape, F32), pltpu.VMEM((R_HEADS, T, T), F32)]
    return pl.pallas_call(
        functools.partial(_retention_kernel, T=T, nsub=nsub, emit_state=emit_state),
        grid=(nb, nc),
        in_specs=in_specs,
        out_specs=out_specs,
        out_shape=out_shape,
        scratch_shapes=scratch,
        compiler_params=pltpu.CompilerParams(
            dimension_semantics=("arbitrary", "arbitrary"),
            vmem_limit_bytes=VMEM_LIMIT_BYTES),
        name="retention_state" if emit_state else "retention",
    )(q, k, v, gs, gn, s0)


def _rms(x, g):
    return x * lax.rsqrt(jnp.mean(x * x, axis=-1, keepdims=True) + EPS) * g


def _out_ffn_kernel(hm_ref, hr_ref, ga_ref, gb_ref, x_ref, wpm_ref, wpr_ref, wout_ref,
                    wup_ref, wdn_ref, g1_ref, g2_ref, g3_ref, o_ref):
    ya = jnp.dot(hm_ref[...], wpm_ref[...], preferred_element_type=F32)
    yb = jnp.dot(hr_ref[...], wpr_ref[...], preferred_element_type=F32)
    mixin = ga_ref[...].astype(F32) * ya + gb_ref[...].astype(F32) * yb
    mix = jnp.dot(mixin.astype(BF16), wout_ref[...], preferred_element_type=F32)
    h1 = x_ref[...] + _rms(mix, g1_ref[...])
    u = _rms(h1, g2_ref[...]).astype(BF16)
    f = None
    for j in range(D_FF // FF_TILE):
        sl = slice(j * FF_TILE, (j + 1) * FF_TILE)
        a = jnp.maximum(jnp.dot(u, wup_ref[:, sl], preferred_element_type=F32), 0.0)
        part = jnp.dot((a * a).astype(BF16), wdn_ref[sl, :], preferred_element_type=F32)
        f = part if f is None else f + part
    o_ref[...] = h1 + _rms(f, g3_ref[...])


def _out_ffn(hm, hr, ga, gb, x2d, wpm, wpr, wout, wup, wdn, g1, g2, g3, tm):
    rows = x2d.shape[0]
    row_spec = lambda w: pl.BlockSpec((tm, w), lambda i: (i, 0))
    consts = (wpm, wpr, wout, wup, wdn, g1, g2, g3)
    return pl.pallas_call(
        _out_ffn_kernel,
        grid=(rows // tm,),
        in_specs=[row_spec(M_INNER), row_spec(R_V), row_spec(D_MODEL), row_spec(D_MODEL),
                  row_spec(D_MODEL), *[_const_spec(c.shape) for c in consts]],
        out_specs=row_spec(D_MODEL),
        out_shape=jax.ShapeDtypeStruct((rows, D_MODEL), F32),
        compiler_params=pltpu.CompilerParams(
            dimension_semantics=("arbitrary",),
            vmem_limit_bytes=VMEM_LIMIT_BYTES),
        name="out_ffn",
    )(hm, hr, ga, gb, x2d, *consts)


def _block_diag_tiles(w):
    rows = w.reshape(M_INNER, M_QKV_BLOCK)
    wide = jnp.tile(rows, (1, BD_TILE // M_QKV_BLOCK))
    r_blk = (lax.broadcasted_iota(jnp.int32, wide.shape, 0) % BD_TILE) // M_QKV_BLOCK
    c_blk = lax.broadcasted_iota(jnp.int32, wide.shape, 1) // M_QKV_BLOCK
    t = jnp.where(r_blk == c_blk, wide, 0.0).astype(BF16)
    return t.reshape(M_INNER // BD_TILE, BD_TILE, BD_TILE)


def _rope_tables(pos0, n):
    half = R_QK_DIM // 2
    inv = ROPE_BASE ** (-np.arange(half, dtype=np.float64) / half)
    ang = (pos0 + np.arange(n, dtype=np.float64))[:, None] * inv[None, :]
    return jnp.asarray(np.cos(ang), F32), jnp.asarray(np.sin(ang), F32)


def kernel(x, meta_tokens, norm_mix_pre, w_in, conv_w, conv_b, w_q_m, w_k_m, w_v_m, w_if, b_if,
           skip_m, gn_m, gn_r, w_proj_m, w_proj_r, w_out, norm_mix_post, norm_ffn_pre, w_up,
           w_down, norm_ffn_post):
    nb, seq, _ = x.shape
    assert norm_mix_pre.shape[0] == 1, "single-layer block"
    assert seq % PROJ_TM == 0 and seq % PREP_TP == 0 and seq % (RET_STEP_CHUNKS * CHUNK_T) == 0
    row = lambda a: a[0].reshape(1, -1).astype(F32)
    n_pad = META_T - N_META

    w_in_bf = (norm_mix_pre[0].astype(F32)[:, None] * w_in[0]).astype(BF16)
    cw, cb = conv_w[0].astype(F32), row(conv_b)

    meta_rows = jnp.concatenate(
        [jnp.zeros((n_pad, D_MODEL), x.dtype), meta_tokens.astype(x.dtype)], axis=0)
    cos_m, sin_m = _rope_tables(-n_pad, META_T)
    cos_x, sin_x = _rope_tables(N_META, seq)
    tail_zero = jnp.zeros((SUBLANES, M_INNER), F32)
    m_meta, r_meta, _ = _in_proj_all(meta_rows, w_in_bf, cw, cb, tail_zero, cos_m, sin_m,
                                     META_T, 1, n_pad)
    x2d = x.reshape(nb * seq, D_MODEL)
    m_in, r_in, (ga, gb) = _in_proj_all(x2d, w_in_bf, cw, cb, m_meta[3], cos_x, sin_x,
                                        PROJ_TM, seq // PROJ_TM, 0)

    nblk = M_INNER // M_QKV_BLOCK
    wif3 = w_if[0].reshape(3, nblk, M_QKV_BLOCK, 2 * M_HEADS)
    fold = lambda w, part: jnp.einsum('nio,noj->nij', w, part).reshape(M_INNER, 2 * M_HEADS)
    wg = jnp.concatenate([fold(w_q_m[0], wif3[0]) + fold(w_k_m[0], wif3[1]), fold(w_v_m[0], wif3[2])],
                         axis=0)
    wif = jnp.zeros((2 * M_INNER, GATE_PAD), F32).at[:, :2 * M_HEADS].set(wg).astype(BF16)
    bif = jnp.zeros((1, LANES), F32).at[0, :2 * M_HEADS].set(b_if[0])
    p_weights = (_block_diag_tiles(w_q_m[0]), _block_diag_tiles(w_k_m[0]),
                 _block_diag_tiles(w_v_m[0]), wif, bif)
    m_zero = (jnp.zeros((M_HEADS, M_HEAD_DIM, M_HEAD_DIM), F32),
              jnp.zeros((M_HEADS, SUBLANES, M_HEAD_DIM), F32),
              jnp.zeros((M_HEADS, SUBLANES, LANES), F32))
    as_seq = lambda a, n, t: a.reshape(n, t, a.shape[-1])
    xm_m, xc_m, zs_m = [as_seq(a, 1, META_T) for a in m_meta[:3]]
    xm_x, xc_x, zs_x = [as_seq(a, nb, seq) for a in m_in[:3]]
    prep_m = _mlstm_prep(xm_m, xc_m, p_weights, META_T, META_T, n_pad)
    m_init = _mlstm(*prep_m[:3], xc_m, zs_m, *prep_m[3:], row(skip_m), row(gn_m), m_zero,
                    META_T, True)[1:]
    prep_x = _mlstm_prep(xm_x, xc_x, p_weights, PREP_TP, CHUNK_T, 0)
    hm = _mlstm(*prep_x[:3], xc_x, zs_x, *prep_x[3:], row(skip_m), row(gn_m), tuple(m_init),
                CHUNK_T, False)[0]

    r_zero = jnp.zeros((R_HEADS, R_QK_DIM, R_V_DIM), F32)
    r_init = _retention(*[as_seq(a, 1, META_T) for a in r_meta], row(gn_r), r_zero,
                        META_T, True)[1]
    hr = _retention(*[as_seq(a, nb, seq) for a in r_in], row(gn_r), r_init, CHUNK_T, False)[0]

    out = _out_ffn(hm.reshape(nb * seq, M_INNER), hr.reshape(nb * seq, R_V), ga, gb, x2d,
                   w_proj_m[0].astype(BF16), w_proj_r[0].astype(BF16), w_out[0].astype(BF16),
                   w_up[0].astype(BF16), w_down[0].astype(BF16),
                   row(norm_mix_post), row(norm_ffn_pre), row(norm_ffn_post), OUT_TM)
    return out.reshape(nb, seq, D_MODEL)
```
